```python
import math
import numpy as np
import jax
import jax.numpy as jnp
from jax import lax

D_MODEL = 1024
BATCH = 2
SEQ = 8192
DEPTH = 4
DEC_BATCH = 32
DEC_SEQ = 16
PAST_LEN = 2048

CHUNK = 64
WINDOW = 128
N_PREV = WINDOW // CHUNK
A_HEADS = 8
A_KV_HEADS = 2
A_GROUP = A_HEADS // A_KV_HEADS
A_HEAD_DIM = 64
A_SCALE = A_HEAD_DIM ** -0.5
B_HEADS = 4
B_KEY_DIM = 128
B_VAL_DIM = 128
C_HEADS = 4
C_KEY_DIM = 128
C_VAL_DIM = 128
C_CONV = 4
D_FF = 2816
F_CONV = 3
N_BRANCH = 3
A_WIDTH = A_HEADS * A_HEAD_DIM
B_WIDTH = B_HEADS * B_VAL_DIM
C_WIDTH = C_HEADS * C_VAL_DIM
D_MIX = A_WIDTH + B_WIDTH + C_WIDTH
C_QKV = C_HEADS * (2 * C_KEY_DIM + C_VAL_DIM)
IN_SPLITS = (A_WIDTH, A_KV_HEADS * A_HEAD_DIM, A_KV_HEADS * A_HEAD_DIM,
             B_HEADS * B_KEY_DIM, B_HEADS * B_KEY_DIM, B_WIDTH, B_WIDTH,
             C_QKV, C_WIDTH, C_HEADS, C_HEADS, N_BRANCH * D_MODEL)
D_IN = sum(IN_SPLITS)
ALPHA = (2 * DEPTH) ** 0.25
BETA = (8 * DEPTH) ** -0.25
LN_EPS = 1e-5
RMS_EPS = 1e-6
NEG_BIG = -1e30
F_MIN = 1e-30
F32 = jnp.float32

kernel_name = 'hybrid_streaming_encoder_step'


def layer_norm(x, g, b):
    xf = x.astype(F32)
    mu = xf.mean(-1, keepdims=True)
    var = jnp.square(xf - mu).mean(-1, keepdims=True)
    return ((xf - mu) * lax.rsqrt(var + LN_EPS) * g.astype(F32) + b.astype(F32)).astype(x.dtype)


def rms_norm_gated(o, w, z):
    n = o * lax.rsqrt(jnp.mean(o * o, -1, keepdims=True) + RMS_EPS)
    return n * w.astype(F32) * jax.nn.silu(z)


def l2norm(a):
    return a * lax.rsqrt(jnp.sum(a * a, -1, keepdims=True) + 1e-6)


def masked_exp(d, mask):
    return jnp.where(mask, jnp.exp(jnp.where(mask, d, 0.0)), 0.0)


def causal_dwconv(u, buf, w):
    width = w.shape[0]
    t = u.shape[1]
    full = jnp.concatenate([buf.astype(u.dtype), u], axis=1)
    out = full[:, 0:t] * w[0]
    for j in range(1, width):
        out = out + full[:, j:j + t] * w[j]
    return out, full[:, t:]


def sink_softmax(s, sink):
    m = jnp.maximum(s.max(-1, keepdims=True), sink)
    e = jnp.exp(s - m)
    return e / (e.sum(-1, keepdims=True) + jnp.exp(sink - m))


def swa_prompt(q, k, v, sink):
    bsz, t = q.shape[:2]
    nc = t // CHUNK
    band_len = (N_PREV + 1) * CHUNK
    pad = ((0, 0), (N_PREV * CHUNK, 0), (0, 0), (0, 0))

    def band(a):
        ap = jnp.pad(a, pad).reshape(bsz, nc + N_PREV, CHUNK, A_KV_HEADS, A_HEAD_DIM)
        return jnp.concatenate([ap[:, j:j + nc] for j in range(N_PREV + 1)], axis=2)

    kb, vb = band(k), band(v)
    qc = q.reshape(bsz, nc, CHUNK, A_KV_HEADS, A_GROUP, A_HEAD_DIM)
    s = jnp.einsum('bnqgrd,bnkgd->bngrqk', qc, kb, preferred_element_type=F32) * A_SCALE
    key_pos = (jnp.arange(nc)[:, None] - N_PREV) * CHUNK + jnp.arange(band_len)[None, :]
    s = jnp.where((key_pos >= 0)[None, :, None, None, None, :], s, NEG_BIG)
    p = sink_softmax(s, sink[:, :, None, None])
    o = jnp.einsum('bngrqk,bnkgd->bnqgrd', p.astype(v.dtype), vb)
    return o.reshape(bsz, t, A_WIDTH)


def swa_sample(q, k, v, k_cache, v_cache, sink):
    bsz, t = q.shape[:2]
    rows = k_cache.shape[1]
    kk = jnp.concatenate([k_cache.astype(k.dtype), k], axis=1)
    vv = jnp.concatenate([v_cache.astype(v.dtype), v], axis=1)
    s = jnp.einsum('bqgrd,bkgd->bgrqk', q, kk, preferred_element_type=F32) * A_SCALE
    p = sink_softmax(s, sink[:, :, None, None])
    o = jnp.einsum('bgrqk,bkgd->bqgrd', p.astype(vv.dtype), vv)
    return o.reshape(bsz, t, A_WIDTH), kk[:, -rows:], vv[:, -rows:]


def gla_chunk(state, inp):
    q, k, v, lf = inp
    c = q.shape[2]
    cum = jnp.cumsum(lf, axis=2)
    incl = jnp.arange(c)[:, None] >= jnp.arange(c)[None, :]
    diff = cum[:, :, :, None, :] - cum[:, :, None, :, :]
    dec = masked_exp(diff, incl[None, None, :, :, None])
    att = jnp.einsum('bhtsk,bhsk->bhts', dec * q[:, :, :, None, :], k)
    o = jnp.einsum('bhtk,bhkv->bhtv', q * jnp.exp(cum), state) + jnp.einsum('bhts,bhsv->bhtv', att, v)
    last = cum[:, :, -1:, :]
    new_state = jnp.exp(last[:, :, 0, :, None]) * state + jnp.einsum('bhsk,bhsv->bhkv', k * jnp.exp(last - cum), v)
    return new_state, o


def gdn_chunk(state, inp):
    q, k, v, beta, g = inp
    c = q.shape[2]
    dv = v.shape[-1]
    cum = jnp.cumsum(g, axis=-1)
    diff = cum[..., :, None] - cum[..., None, :]
    idx = jnp.arange(c)
    incl = idx[:, None] >= idx[None, :]
    strict = idx[:, None] > idx[None, :]
    dec_incl = masked_exp(diff, incl)
    dec_strict = jnp.where(strict, dec_incl, 0.0)
    a = beta[..., :, None] * jnp.einsum('bhtk,bhsk->bhts', k, k) * dec_strict + jnp.eye(c, dtype=F32)
    rhs = jnp.concatenate([v * beta[..., None], k * (beta * jnp.exp(cum))[..., None]], axis=-1)
    sol = lax.linalg.triangular_solve(a, rhs, left_side=True, lower=True, unit_diagonal=True)
    w = sol[..., :dv] - jnp.einsum('bhtk,bhkv->bhtv', sol[..., dv:], state)
    qk = jnp.einsum('bhtk,bhsk->bhts', q, k) * dec_incl
    o = jnp.einsum('bhtk,bhkv->bhtv', q * jnp.exp(cum)[..., None], state) + jnp.einsum('bhts,bhsv->bhtv', qk, w)
    last = cum[..., -1:]
    new_state = jnp.exp(last)[..., None] * state + jnp.einsum('bhsk,bhsv->bhkv', k * jnp.exp(last - cum)[..., None], w)
    return new_state, o


def run_chunks(step, state, seqs):
    t = seqs[0].shape[2]
    c = min(CHUNK, t)
    n = t // c

    def split(a):
        return jnp.moveaxis(a.reshape(a.shape[:2] + (n, c) + a.shape[3:]), 2, 0)

    final, outs = lax.scan(step, state, tuple(split(a) for a in seqs))
    outs = jnp.moveaxis(outs, 0, 2)
    return outs.reshape(outs.shape[:2] + (t,) + outs.shape[4:]), final


def trunk_layer(x, state, params, prompt):
    k_cache, v_cache, s_hgrn, s_gdn, buf_gdn, buf_ffn = state
    (w_in, sink, lb, hgrn_w, gdn_conv_w, a_log, dt_bias, gdn_w, w_branch, w_out,
     ln1_g, ln1_b, w_up, f_conv_w, f_conv_b, w_down, ln2_g, ln2_b) = params
    bsz, t, _ = x.shape

    def heads(a, d):
        return a.reshape(bsz, t, -1, d).transpose(0, 2, 1, 3).astype(F32)

    def merge_heads(o):
        return o.transpose(0, 2, 1, 3).reshape(bsz, t, -1).astype(x.dtype)

    offsets = [int(o) for o in np.cumsum(IN_SPLITS)[:-1]]
    aq, ak, av, bq, bf, bi, bg, cqkv, cz, cb, ca, gt = jnp.split(x @ w_in, offsets, axis=-1)

    q_a = aq.reshape(bsz, t, A_KV_HEADS, A_GROUP, A_HEAD_DIM)
    k_a = ak.reshape(bsz, t, A_KV_HEADS, A_HEAD_DIM)
    v_a = av.reshape(bsz, t, A_KV_HEADS, A_HEAD_DIM)
    sink_a = sink.reshape(A_KV_HEADS, A_GROUP).astype(F32)
    if prompt:
        o_a = swa_prompt(q_a, k_a, v_a, sink_a)
        new_k, new_v = k_a[:, -WINDOW:], v_a[:, -WINDOW:]
    else:
        o_a, new_k, new_v = swa_sample(q_a, k_a, v_a, k_cache, v_cache, sink_a)

    f_pre = heads(bf, B_KEY_DIM)
    lbh = lb.reshape(B_HEADS, 1, B_KEY_DIM).astype(F32)
    sig_f = jax.nn.sigmoid(f_pre)
    log_f = jnp.log(jnp.maximum(lbh + (1.0 - lbh) * sig_f, F_MIN))
    k_b = (1.0 - lbh) * (1.0 - sig_f)
    q_b = jax.nn.silu(heads(bq, B_KEY_DIM))
    v_b = heads(bi, B_VAL_DIM)
    o_b, new_hgrn = run_chunks(gla_chunk, s_hgrn.astype(F32), (q_b, k_b, v_b, log_f))
    o_b = merge_heads(rms_norm_gated(o_b, hgrn_w, heads(bg, B_VAL_DIM)))

    u, new_gdn_conv = causal_dwconv(cqkv, buf_gdn, gdn_conv_w)
    u = jax.nn.silu(u)
    cq, ck, cv = jnp.split(u, [C_HEADS * C_KEY_DIM, 2 * C_HEADS * C_KEY_DIM], axis=-1)
    q_c = l2norm(heads(cq, C_KEY_DIM)) * (C_KEY_DIM ** -0.5)
    k_c = l2norm(heads(ck, C_KEY_DIM))
    v_c = heads(cv, C_VAL_DIM)
    beta = jax.nn.sigmoid(cb.astype(F32)).transpose(0, 2, 1)
    g = (-jnp.exp(a_log.astype(F32)) * jax.nn.softplus(ca.astype(F32) + dt_bias.astype(F32))).transpose(0, 2, 1)
    o_c, new_gdn = run_chunks(gdn_chunk, s_gdn.astype(F32), (q_c, k_c, v_c, beta, g))
    o_c = merge_heads(rms_norm_gated(o_c, gdn_w, heads(cz, C_VAL_DIM)))

    gates = jax.nn.sigmoid(gt.reshape(bsz, t, N_BRANCH, D_MODEL))
    w_ba, w_bb, w_bc = jnp.split(w_branch, [A_WIDTH, A_WIDTH + B_WIDTH], axis=0)
    mix = gates[:, :, 0] * (o_a @ w_ba) + gates[:, :, 1] * (o_b @ w_bb) + gates[:, :, 2] * (o_c @ w_bc)
    x = layer_norm(ALPHA * x + mix @ w_out, ln1_g, ln1_b)

    up, new_ffn_conv = causal_dwconv(x @ w_up, buf_ffn, f_conv_w)
    gate, val = jnp.split(up + f_conv_b, 2, axis=-1)
    x = layer_norm(ALPHA * x + (jax.nn.silu(gate) * val) @ w_down, ln2_g, ln2_b)
    return x, (new_k, new_v, new_hgrn, new_gdn, new_gdn_conv, new_ffn_conv)


def setup_inputs(seed: int = 0) -> dict:
    key = jax.random.key(seed)
    ks = jax.random.split(key, 32)

    def nrm(k, shape, s):
        return jax.random.normal(k, shape, F32) * s

    rows = min(WINDOW, PAST_LEN)
    dt = jnp.exp(jax.random.uniform(ks[14], (DEPTH, C_HEADS), F32, math.log(1e-3), math.log(1e-1)))
    return {
        'x_prompt': nrm(ks[0], (BATCH, SEQ, D_MODEL), 1.0),
        'x_sample': nrm(ks[1], (DEC_BATCH, DEC_SEQ, D_MODEL), 1.0),
        'cache_swa_k': nrm(ks[2], (DEPTH, DEC_BATCH, rows, A_KV_HEADS, A_HEAD_DIM), 1.0),
        'cache_swa_v': nrm(ks[3], (DEPTH, DEC_BATCH, rows, A_KV_HEADS, A_HEAD_DIM), 1.0),
        'state_hgrn': nrm(ks[4], (DEPTH, DEC_BATCH, B_HEADS, B_KEY_DIM, B_VAL_DIM), 0.5),
        'state_gdn': nrm(ks[5], (DEPTH, DEC_BATCH, C_HEADS, C_KEY_DIM, C_VAL_DIM), C_KEY_DIM ** -0.5),
        'state_gdn_conv': nrm(ks[6], (DEPTH, DEC_BATCH, C_CONV - 1, C_QKV), 1.0),
        'state_ffn_conv': nrm(ks[7], (DEPTH, DEC_BATCH, F_CONV - 1, 2 * D_FF), 1.0),
        'ln_in_g': 1.0 + nrm(ks[8], (D_MODEL,), 0.02),
        'ln_in_b': nrm(ks[9], (D_MODEL,), 0.02),
        'w_in': nrm(ks[10], (DEPTH, D_MODEL, D_IN), D_MODEL ** -0.5),
        'attn_sinks': nrm(ks[11], (DEPTH, A_HEADS), 0.5),
        'hgrn_lb_logits': nrm(ks[12], (DEPTH, B_HEADS * B_KEY_DIM), 0.5),
        'hgrn_norm_w': 1.0 + nrm(ks[13], (DEPTH, B_VAL_DIM), 0.02),
        'gdn_conv_w': nrm(ks[15], (DEPTH, C_CONV, C_QKV), C_CONV ** -0.5),
        'gdn_a_log': jnp.log(jax.random.uniform(ks[16], (DEPTH, C_HEADS), F32, 1.0, 16.0)),
        'gdn_dt_bias': dt + jnp.log(-jnp.expm1(-dt)),
        'gdn_norm_w': 1.0 + nrm(ks[17], (DEPTH, C_VAL_DIM), 0.02),
        'w_branch': nrm(ks[18], (DEPTH, D_MIX, D_MODEL), BETA * A_WIDTH ** -0.5),
        'w_out': nrm(ks[19], (DEPTH, D_MODEL, D_MODEL), BETA * D_MODEL ** -0.5),
        'ln1_g': 1.0 + nrm(ks[20], (DEPTH, D_MODEL), 0.02),
        'ln1_b': nrm(ks[21], (DEPTH, D_MODEL), 0.02),
        'w_up': nrm(ks[22], (DEPTH, D_MODEL, 2 * D_FF), D_MODEL ** -0.5),
        'ffn_conv_w': nrm(ks[23], (DEPTH, F_CONV, 2 * D_FF), F_CONV ** -0.5),
        'ffn_conv_b': nrm(ks[24], (DEPTH, 2 * D_FF), 0.02),
        'w_down': nrm(ks[25], (DEPTH, D_FF, D_MODEL), BETA * D_FF ** -0.5),
        'ln2_g': 1.0 + nrm(ks[26], (DEPTH, D_MODEL), 0.02),
        'ln2_b': nrm(ks[27], (DEPTH, D_MODEL), 0.02),
    }


def reference(x_prompt, x_sample, cache_swa_k, cache_swa_v, state_hgrn, state_gdn,
              state_gdn_conv, state_ffn_conv, ln_in_g, ln_in_b, w_in, attn_sinks,
              hgrn_lb_logits, hgrn_norm_w, gdn_conv_w, gdn_a_log, gdn_dt_bias, gdn_norm_w,
              w_branch, w_out, ln1_g, ln1_b, w_up, ffn_conv_w, ffn_conv_b, w_down, ln2_g, ln2_b):
    lb_sm = jax.nn.softmax(hgrn_lb_logits.astype(F32), axis=0)
    lower_bounds = jnp.cumsum(lb_sm, axis=0) - lb_sm[0]

    hp = layer_norm(x_prompt, ln_in_g, ln_in_b)
    hs = layer_norm(x_sample, ln_in_g, ln_in_b)
    bp = x_prompt.shape[0]
    prompt_state = (None, None,
                    jnp.zeros((bp, B_HEADS, B_KEY_DIM, B_VAL_DIM), F32),
                    jnp.zeros((bp, C_HEADS, C_KEY_DIM, C_VAL_DIM), F32),
                    jnp.zeros((bp, C_CONV - 1, C_QKV), x_prompt.dtype),
                    jnp.zeros((bp, F_CONV - 1, 2 * D_FF), x_prompt.dtype))
    p_new = []
    s_new = []
    for l in range(DEPTH):
        params = (w_in[l], attn_sinks[l], lower_bounds[l], hgrn_norm_w[l], gdn_conv_w[l],
                  gdn_a_log[l], gdn_dt_bias[l], gdn_norm_w[l], w_branch[l], w_out[l],
                  ln1_g[l], ln1_b[l], w_up[l], ffn_conv_w[l], ffn_conv_b[l], w_down[l],
                  ln2_g[l], ln2_b[l])
        hp, st_p = trunk_layer(hp, prompt_state, params, True)
        sample_state = (cache_swa_k[l], cache_swa_v[l], state_hgrn[l], state_gdn[l],
                        state_gdn_conv[l], state_ffn_conv[l])
        hs, st_s = trunk_layer(hs, sample_state, params, False)
        p_new.append(st_p)
        s_new.append(st_s)

    def stack(states, i):
        return jnp.stack([st[i] for st in states])

    return (hp, hs,
            stack(p_new, 0), stack(p_new, 1), stack(p_new, 2), stack(p_new, 3), stack(p_new, 4), stack(p_new, 5),
            stack(s_new, 0), stack(s_new, 1), stack(s_new, 2), stack(s_new, 3), stack(s_new, 4), stack(s_new, 5))
```

```python
import functools
import math

import jax
import jax.numpy as jnp
from jax import lax
from jax.experimental import pallas as pl
from jax.experimental.pallas import tpu as pltpu

F32 = jnp.float32
BF16 = jnp.bfloat16

CHUNK = 64
WINDOW = 128
A_HEADS = 8
A_KV_HEADS = 2
A_GROUP = A_HEADS // A_KV_HEADS
A_HEAD_DIM = 64
A_SCALE = A_HEAD_DIM ** -0.5
A_WIDTH = A_HEADS * A_HEAD_DIM
A_KV_WIDTH = A_KV_HEADS * A_HEAD_DIM
B_HEADS = 4
B_DIM = 128
C_HEADS = 4
C_DIM = 128
C_CONV = 4
F_CONV = 3
N_BRANCH = 3
B_WIDTH = B_HEADS * B_DIM
C_WIDTH = C_HEADS * C_DIM
C_QKV = 3 * C_WIDTH
LN_EPS = 1e-5
RMS_EPS = 1e-6
NEG_BIG = -1e30
F_MIN = 1e-30

LANES = 128
SUBLANES = 8
FFN_COLS = 256
VMEM_LIMIT = 48 * 1024 * 1024


def _cparams(*sem):
    return pltpu.CompilerParams(dimension_semantics=sem, vmem_limit_bytes=VMEM_LIMIT)


def _mm(a, b):
    return jnp.dot(a.astype(BF16), b.astype(BF16), preferred_element_type=F32)


def _mm_nt(a, b):
    return lax.dot_general(a.astype(BF16), b.astype(BF16), (((1,), (1,)), ((), ())),
                           preferred_element_type=F32)


def _mm_tn(a, b):
    return lax.dot_general(a.astype(BF16), b.astype(BF16), (((0,), (0,)), ((), ())),
                           preferred_element_type=F32)


def _mm_f32(a, b):
    return jnp.dot(a, b, preferred_element_type=F32, precision=lax.Precision.HIGHEST)


def _mm_nt_f32(a, b):
    return lax.dot_general(a, b, (((1,), (1,)), ((), ())), preferred_element_type=F32,
                           precision=lax.Precision.HIGHEST)


def _sigmoid(x):
    return 1.0 / (1.0 + jnp.exp(-x))


def _silu(x):
    return x * _sigmoid(x)


def _ln_rows(x, g, b):
    mu = jnp.mean(x, axis=-1, keepdims=True)
    xc = x - mu
    var = jnp.mean(xc * xc, axis=-1, keepdims=True)
    return xc * lax.rsqrt(var + LN_EPS) * g + b


def _rms_gated(o, w, z):
    n = o * lax.rsqrt(jnp.mean(o * o, axis=-1, keepdims=True) + RMS_EPS)
    return n * w * _silu(z)


def _tri(c, lower):
    r = lax.broadcasted_iota(jnp.int32, (c, c), 0)
    s = lax.broadcasted_iota(jnp.int32, (c, c), 1)
    return jnp.where(r >= s if lower else r <= s, 1.0, 0.0).astype(F32)


def _ln_kernel(x_ref, g_ref, b_ref, of_ref, ob_ref):
    y = _ln_rows(x_ref[...], g_ref[...], b_ref[...])
    of_ref[...] = y
    ob_ref[...] = y.astype(BF16)


def _layer_norm(x, g, b):
    bsz, t, d = x.shape
    n = bsz * t
    tm = min(512, n)
    of, ob = pl.pallas_call(
        _ln_kernel,
        grid=(n // tm,),
        in_specs=[pl.BlockSpec((tm, d), lambda i: (i, 0)),
                  pl.BlockSpec((1, d), lambda i: (0, 0)),
                  pl.BlockSpec((1, d), lambda i: (0, 0))],
        out_specs=[pl.BlockSpec((tm, d), lambda i: (i, 0)),
                   pl.BlockSpec((tm, d), lambda i: (i, 0))],
        out_shape=[jax.ShapeDtypeStruct((n, d), F32), jax.ShapeDtypeStruct((n, d), BF16)],
        compiler_params=_cparams("parallel"),
        name="input_ln",
    )(x.reshape(n, d), g.reshape(1, d), b.reshape(1, d))
    return of.reshape(bsz, t, d), ob.reshape(bsz, t, d)


def _mm_kernel(x_ref, w_ref, o_ref):
    o_ref[...] = jnp.dot(x_ref[...], w_ref[...], preferred_element_type=F32)


def _matmul(x, w, tn):
    n, k = x.shape
    dout = w.shape[1]
    tm = min(512, n)
    return pl.pallas_call(
        _mm_kernel,
        grid=(dout // tn, n // tm),
        in_specs=[pl.BlockSpec((tm, k), lambda j, i: (i, 0)),
                  pl.BlockSpec((k, tn), lambda j, i: (0, j))],
        out_specs=pl.BlockSpec((tm, tn), lambda j, i: (i, j)),
        out_shape=jax.ShapeDtypeStruct((n, dout), F32),
        compiler_params=_cparams("parallel", "parallel"),
        name="in_proj",
    )(x, w)


def _swa_kernel(q_ref, kvc_ref, kp_ref, vp_ref, sink_ref, o_ref, *cache_refs, prompt, qb):
    kvc = kvc_ref[...]
    kp = kp_ref[...]
    vp = vp_ref[...]
    nkeys = WINDOW + qb
    if prompt:
        row = lax.broadcasted_iota(jnp.int32, (qb, nkeys), 0)
        col = lax.broadcasted_iota(jnp.int32, (qb, nkeys), 1)
        behind = row // CHUNK - (col // CHUNK - WINDOW // CHUNK)
        first_valid = jnp.where(pl.program_id(1) == 0, WINDOW, 0)
        ok = (behind >= 0) & (behind <= WINDOW // CHUNK) & (col >= first_valid)
    for kv in range(A_KV_HEADS):
        ks = slice(kv * A_HEAD_DIM, (kv + 1) * A_HEAD_DIM)
        vs = slice(A_KV_WIDTH + kv * A_HEAD_DIM, A_KV_WIDTH + (kv + 1) * A_HEAD_DIM)
        kk = jnp.concatenate([kp[:, ks], kvc[:, ks]], axis=0).astype(BF16)
        vv = jnp.concatenate([vp[:, ks], kvc[:, vs]], axis=0).astype(BF16)
        for g in range(A_GROUP):
            h = kv * A_GROUP + g
            hs = slice(h * A_HEAD_DIM, (h + 1) * A_HEAD_DIM)
            s = _mm_nt(q_ref[:, hs], kk) * A_SCALE
            if prompt:
                s = jnp.where(ok, s, NEG_BIG)
            sink = sink_ref[h]
            m = jnp.maximum(jnp.max(s, axis=-1, keepdims=True), sink)
            e = jnp.exp(s - m)
            den = jnp.sum(e, axis=-1, keepdims=True) + jnp.exp(sink - m)
            o_ref[:, hs] = (_mm(e, vv) / den).astype(o_ref.dtype)
    if not prompt:
        nk_ref, nv_ref = cache_refs
        keep = WINDOW - qb
        nk_ref[0:keep, :] = kp[qb:, :]
        nk_ref[keep:, :] = kvc[:, 0:A_KV_WIDTH]
        nv_ref[0:keep, :] = vp[qb:, :]
        nv_ref[keep:, :] = kvc[:, A_KV_WIDTH:]


def _swa(sa, k_cache, v_cache, sink, prompt):
    bsz, t, _ = sa.shape
    qcol = 0
    kvcol = A_WIDTH // (2 * A_KV_WIDTH)
    if prompt:
        qb = WINDOW
        assert t % qb == 0
        kcol = A_WIDTH // A_KV_WIDTH
        kp_spec = pl.BlockSpec((None, WINDOW, A_KV_WIDTH), lambda b, i: (b, jnp.maximum(i - 1, 0), kcol))
        vp_spec = pl.BlockSpec((None, WINDOW, A_KV_WIDTH), lambda b, i: (b, jnp.maximum(i - 1, 0), kcol + 1))
        kp_arr, vp_arr = sa, sa
        out_shape = [jax.ShapeDtypeStruct((bsz, t, A_WIDTH), BF16)]
        out_specs = [pl.BlockSpec((None, qb, A_WIDTH), lambda b, i: (b, i, 0))]
    else:
        qb = t
        assert t <= WINDOW and t % (2 * SUBLANES) == 0 and k_cache.shape[1] == WINDOW
        kp_spec = pl.BlockSpec((None, WINDOW, A_KV_WIDTH), lambda b, i: (b, 0, 0))
        vp_spec = kp_spec
        kp_arr = k_cache.reshape(bsz, WINDOW, A_KV_WIDTH)
        vp_arr = v_cache.reshape(bsz, WINDOW, A_KV_WIDTH)
        out_shape = [jax.ShapeDtypeStruct((bsz, t, A_WIDTH), BF16),
                     jax.ShapeDtypeStruct((bsz, WINDOW, A_KV_WIDTH), F32),
                     jax.ShapeDtypeStruct((bsz, WINDOW, A_KV_WIDTH), F32)]
        cache_spec = pl.BlockSpec((None, WINDOW, A_KV_WIDTH), lambda b, i: (b, 0, 0))
        out_specs = [pl.BlockSpec((None, qb, A_WIDTH), lambda b, i: (b, i, 0)), cache_spec, cache_spec]
    outs = pl.pallas_call(
        functools.partial(_swa_kernel, prompt=prompt, qb=qb),
        grid=(bsz, t // qb),
        in_specs=[pl.BlockSpec((None, qb, A_WIDTH), lambda b, i: (b, i, qcol)),
                  pl.BlockSpec((None, qb, 2 * A_KV_WIDTH), lambda b, i: (b, i, kvcol)),
                  kp_spec, vp_spec,
                  pl.BlockSpec(memory_space=pltpu.SMEM)],
        out_specs=out_specs,
        out_shape=out_shape,
        compiler_params=_cparams("parallel", "parallel"),
        name="swa",
    )(sa, sa, kp_arr, vp_arr, sink)
    if prompt:
        new_k = sa[:, t - WINDOW:, A_WIDTH:A_WIDTH + A_KV_WIDTH]
        new_v = sa[:, t - WINDOW:, A_WIDTH + A_KV_WIDTH:]
    else:
        new_k, new_v = outs[1], outs[2]
    shape = (bsz, WINDOW, A_KV_HEADS, A_HEAD_DIM)
    return outs[0], new_k.reshape(shape), new_v.reshape(shape)


def _block_refs(cum, c):
    row = lax.broadcasted_iota(jnp.int32, cum.shape, 0)
    refs = {}
    end = cum
    h = 1
    while h < c:
        if h > 1:
            nxt = pltpu.roll(end, c - h // 2, axis=0)
            end = jnp.where((row % h) >= (h // 2), end, nxt)
        start = jnp.where(row >= h, pltpu.roll(end, h, axis=0), 0.0)
        refs[h] = (end, start)
        h *= 2
    return refs


def _gla_attention(q, k, cum, c):
    r = lax.broadcasted_iota(jnp.int32, (c, c), 0)
    s = lax.broadcasted_iota(jnp.int32, (c, c), 1)
    x = r ^ s
    att = jnp.where(r == s, jnp.sum(q * k, axis=-1, keepdims=True), 0.0)
    for h, (end, start) in _block_refs(cum, c).items():
        qh = q * jnp.exp(cum - start)
        kh = k * jnp.exp(end - cum)
        att = jnp.where((r > s) & (x >= h) & (x < 2 * h), _mm_nt(qh, kh), att)
    return att


def _hgrn_kernel(x_ref, lb_ref, w_ref, s0_ref, o_ref, s_ref, st_ref, *, c):
    ci = pl.program_id(1)

    @pl.when(ci == 0)
    def _():
        for h in range(B_HEADS):
            st_ref[h] = s0_ref[h].T

    ltri = _tri(c, True)
    for h in range(B_HEADS):
        cs = slice(h * B_DIM, (h + 1) * B_DIM)
        qp = x_ref[:, cs]
        fp = x_ref[:, B_WIDTH + h * B_DIM:B_WIDTH + (h + 1) * B_DIM]
        v = x_ref[:, 2 * B_WIDTH + h * B_DIM:2 * B_WIDTH + (h + 1) * B_DIM]
        gp = x_ref[:, 3 * B_WIDTH + h * B_DIM:3 * B_WIDTH + (h + 1) * B_DIM]
        lb = lb_ref[:, cs]
        sig = _sigmoid(fp)
        lf = jnp.log(jnp.maximum(lb + (1.0 - lb) * sig, F_MIN))
        k = (1.0 - lb) * (1.0 - sig)
        q = _silu(qp)
        cum = _mm_f32(ltri, lf)
        att = _gla_attention(q, k, cum, c)
        st = st_ref[h]
        o = _mm_nt(q * jnp.exp(cum), st) + _mm(att, v)
        last = cum[c - 1:c, :]
        st_ref[h] = st * jnp.exp(last) + _mm_tn(v, k * jnp.exp(last - cum))
        o_ref[:, cs] = _rms_gated(o, w_ref[...], gp).astype(o_ref.dtype)

    @pl.when(ci == pl.num_programs(1) - 1)
    def _():
        for h in range(B_HEADS):
            s_ref[h] = st_ref[h].T


def _hgrn(sb, lb, norm_w, state):
    bsz, t, _ = sb.shape
    c = min(CHUNK, t)
    st_spec = pl.BlockSpec((None, B_HEADS, B_DIM, B_DIM), lambda b, i: (b, 0, 0, 0))
    return pl.pallas_call(
        functools.partial(_hgrn_kernel, c=c),
        grid=(bsz, t // c),
        in_specs=[pl.BlockSpec((None, c, 4 * B_WIDTH), lambda b, i: (b, i, 0)),
                  pl.BlockSpec((1, B_WIDTH), lambda b, i: (0, 0)),
                  pl.BlockSpec((1, B_DIM), lambda b, i: (0, 0)),
                  st_spec],
        out_specs=[pl.BlockSpec((None, c, B_WIDTH), lambda b, i: (b, i, 0)), st_spec],
        out_shape=[jax.ShapeDtypeStruct((bsz, t, B_WIDTH), BF16),
                   jax.ShapeDtypeStruct((bsz, B_HEADS, B_DIM, B_DIM), F32)],
        scratch_shapes=[pltpu.VMEM((B_HEADS, B_DIM, B_DIM), F32)],
        compiler_params=_cparams("parallel", "arbitrary"),
        name="hgrn",
    )(sb, lb.reshape(1, B_WIDTH), norm_w.reshape(1, B_DIM), state)


def _unit_lower_inverse(n, c):
    r = lax.broadcasted_iota(jnp.int32, (c, c), 0)
    s = lax.broadcasted_iota(jnp.int32, (c, c), 1)
    inv = jnp.where(r == s, 1.0, 0.0) - n
    p = n
    for _ in range(int(math.log2(c)) - 1):
        p = _mm_f32(p, p)
        inv = inv + _mm_f32(inv, p)
    return inv


def _gdn_kernel(x_ref, cw_ref, alog_ref, dtb_ref, w_ref, s0_ref, b0_ref,
                o_ref, s_ref, b_ref, st_ref, cbuf_ref, *, c):
    ci = pl.program_id(1)
    pad = SUBLANES
    keep = C_CONV - 1

    @pl.when(ci == 0)
    def _():
        st_ref[...] = s0_ref[...]
        cbuf_ref[pad - keep:pad, :] = b0_ref[...]

    cbuf_ref[pad:pad + c, :] = x_ref[:, 0:C_QKV]
    u = cbuf_ref[pad - keep:pad - keep + c, :] * cw_ref[0:1, :]
    for j in range(1, C_CONV):
        u = u + cbuf_ref[pad - keep + j:pad - keep + j + c, :] * cw_ref[j:j + 1, :]
    tail = cbuf_ref[pad + c - keep:pad + c, :]
    cbuf_ref[pad - keep:pad, :] = tail

    @pl.when(ci == pl.num_programs(1) - 1)
    def _():
        b_ref[...] = tail

    u = _silu(u)
    small = x_ref[:, C_QKV + C_WIDTH:C_QKV + C_WIDTH + LANES]
    pre = small + dtb_ref[...]
    softplus = jnp.maximum(pre, 0.0) + jnp.log(1.0 + jnp.exp(-jnp.abs(pre)))
    g_all = -jnp.exp(alog_ref[...]) * softplus
    beta_all = _sigmoid(small)
    cum_all = _mm_f32(_tri(c, True), g_all)
    sel_r = lax.broadcasted_iota(jnp.int32, (SUBLANES, LANES), 0)
    sel_c = lax.broadcasted_iota(jnp.int32, (SUBLANES, LANES), 1)
    sel = jnp.where(sel_r == sel_c, 1.0, 0.0).astype(F32)
    cum_rows = _mm_nt_f32(sel, cum_all)
    r = lax.broadcasted_iota(jnp.int32, (c, c), 0)
    s = lax.broadcasted_iota(jnp.int32, (c, c), 1)
    incl = r >= s
    for h in range(C_HEADS):
        cs = slice(h * C_DIM, (h + 1) * C_DIM)
        uq = u[:, cs]
        uk = u[:, C_WIDTH + h * C_DIM:C_WIDTH + (h + 1) * C_DIM]
        v = u[:, 2 * C_WIDTH + h * C_DIM:2 * C_WIDTH + (h + 1) * C_DIM]
        z = x_ref[:, C_QKV + h * C_DIM:C_QKV + (h + 1) * C_DIM]
        q = uq * lax.rsqrt(jnp.sum(uq * uq, axis=-1, keepdims=True) + 1e-6) * (C_DIM ** -0.5)
        k = uk * lax.rsqrt(jnp.sum(uk * uk, axis=-1, keepdims=True) + 1e-6)
        beta = beta_all[:, h:h + 1]
        cum = cum_all[:, C_HEADS + h:C_HEADS + h + 1]
        cum_row = cum_rows[C_HEADS + h:C_HEADS + h + 1, :]
        dec = jnp.where(incl, jnp.exp(jnp.where(incl, cum - cum_row, 0.0)), 0.0)
        kb = k.astype(BF16)
        n = jnp.where(r > s, beta * _mm_nt(kb, kb) * dec, 0.0)
        inv = _unit_lower_inverse(n, c)
        ecum = jnp.exp(cum)
        rhs = jnp.concatenate([v * beta, k * (beta * ecum)], axis=-1)
        sol = _mm_f32(inv, rhs)
        st = st_ref[h]
        stb = st.astype(BF16)
        w = sol[:, 0:C_DIM] - _mm(sol[:, C_DIM:], stb)
        qk = _mm_nt(q, kb) * dec
        o = _mm(q * ecum, stb) + _mm(qk, w)
        last = cum[c - 1:c, :]
        st_ref[h] = jnp.exp(last) * st + _mm_tn(k * jnp.exp(last - cum), w)
        o_ref[:, cs] = _rms_gated(o, w_ref[...], z).astype(o_ref.dtype)

    @pl.when(ci == pl.num_programs(1) - 1)
    def _():
        s_ref[...] = st_ref[...]


def _gdn(sc, conv_w, a_log, dt_bias, norm_w, state, buf):
    bsz, t, width = sc.shape
    c = min(CHUNK, t)
    keep = C_CONV - 1
    pad_vec = lambda a: jnp.zeros((1, LANES), F32).at[0, C_HEADS:2 * C_HEADS].set(a.astype(F32))
    st_spec = pl.BlockSpec((None, C_HEADS, C_DIM, C_DIM), lambda b, i: (b, 0, 0, 0))
    buf_spec = pl.BlockSpec((None, keep, C_QKV), lambda b, i: (b, 0, 0))
    const = lambda shape: pl.BlockSpec(shape, lambda b, i: (0, 0))
    return pl.pallas_call(
        functools.partial(_gdn_kernel, c=c),
        grid=(bsz, t // c),
        in_specs=[pl.BlockSpec((None, c, width), lambda b, i: (b, i, 0)),
                  const((C_CONV, C_QKV)), const((1, LANES)), const((1, LANES)), const((1, C_DIM)),
                  st_spec, buf_spec],
        out_specs=[pl.BlockSpec((None, c, C_WIDTH), lambda b, i: (b, i, 0)), st_spec, buf_spec],
        out_shape=[jax.ShapeDtypeStruct((bsz, t, C_WIDTH), BF16),
                   jax.ShapeDtypeStruct((bsz, C_HEADS, C_DIM, C_DIM), F32),
                   jax.ShapeDtypeStruct((bsz, keep, C_QKV), F32)],
        scratch_shapes=[pltpu.VMEM((C_HEADS, C_DIM, C_DIM), F32),
                        pltpu.VMEM((SUBLANES + c, C_QKV), F32)],
        compiler_params=_cparams("parallel", "arbitrary"),
        name="gdn",
    )(sc, conv_w, pad_vec(a_log), pad_vec(dt_bias), norm_w.reshape(1, C_DIM), state, buf)


def _merge_kernel(oa_ref, ob_ref, oc_ref, g_ref, x_ref, wa_ref, wb_ref, wc_ref, wo_ref,
                  lg_ref, lb_ref, y_ref, *, alpha):
    d = x_ref.shape[-1]
    mix = _sigmoid(g_ref[:, 0:d]) * _mm(oa_ref[...], wa_ref[...])
    mix = mix + _sigmoid(g_ref[:, d:2 * d]) * _mm(ob_ref[...], wb_ref[...])
    mix = mix + _sigmoid(g_ref[:, 2 * d:3 * d]) * _mm(oc_ref[...], wc_ref[...])
    y_ref[...] = _ln_rows(alpha * x_ref[...] + _mm(mix, wo_ref[...]), lg_ref[...], lb_ref[...])


def _merge(oa, ob, oc, gates, x, wba, wbb, wbc, wout, ln_g, ln_b, alpha):
    n, d = x.shape
    tm = min(256, n)
    row = lambda w: pl.BlockSpec((tm, w), lambda i: (i, 0))
    const = lambda a: pl.BlockSpec(a.shape, lambda i: (0, 0))
    return pl.pallas_call(
        functools.partial(_merge_kernel, alpha=alpha),
        grid=(n // tm,),
        in_specs=[row(A_WIDTH), row(B_WIDTH), row(C_WIDTH), row(N_BRANCH * d), row(d),
                  const(wba), const(wbb), const(wbc), const(wout),
                  pl.BlockSpec((1, d), lambda i: (0, 0)), pl.BlockSpec((1, d), lambda i: (0, 0))],
        out_specs=row(d),
        out_shape=jax.ShapeDtypeStruct((n, d), F32),
        compiler_params=_cparams("parallel"),
        name="merge",
    )(oa, ob, oc, gates, x, wba, wbb, wbc, wout, ln_g.reshape(1, d), ln_b.reshape(1, d))


def _ffn_kernel(x_ref, b0_ref, wu_ref, cw_ref, cb_ref, wd_ref, lg_ref, lb_ref,
                yf_ref, yb_ref, b_ref, carry_ref, win_ref, *, alpha, d_ff):
    tb, tt, d = x_ref.shape
    m = tb * tt
    nj = d_ff // FFN_COLS
    keep = F_CONV - 1
    pad = SUBLANES
    ti = pl.program_id(1)

    def state_cols(j):
        return (slice(j * FFN_COLS, (j + 1) * FFN_COLS), slice(d_ff + j * FFN_COLS, d_ff + (j + 1) * FFN_COLS))

    @pl.when(ti == 0)
    def _():
        for j in range(nj):
            gcol, vcol = state_cols(j)
            carry_ref[j, :, :, 0:FFN_COLS] = b0_ref[:, :, gcol]
            carry_ref[j, :, :, FFN_COLS:] = b0_ref[:, :, vcol]

    x = x_ref[...].reshape(m, d)
    xb = x.astype(BF16)
    acc = jnp.zeros((m, d), F32)
    for j in range(nj):
        up = jnp.dot(xb, wu_ref[j], preferred_element_type=F32)
        win_ref[:, pad - keep:pad, :] = carry_ref[j]
        win_ref[:, pad:pad + tt, :] = up.reshape(tb, tt, 2 * FFN_COLS)
        conv = win_ref[:, pad - keep:pad - keep + tt, :] * cw_ref[j, 0:1, :]
        for i in range(1, F_CONV):
            conv = conv + win_ref[:, pad - keep + i:pad - keep + i + tt, :] * cw_ref[j, i:i + 1, :]
        conv = conv + cb_ref[j]
        carry_ref[j] = win_ref[:, pad + tt - keep:pad + tt, :]
        act = (_silu(conv[:, :, 0:FFN_COLS]) * conv[:, :, FFN_COLS:]).reshape(m, FFN_COLS)
        acc = acc + jnp.dot(act.astype(BF16), wd_ref[j], preferred_element_type=F32)
    y = _ln_rows(alpha * x + acc, lg_ref[...], lb_ref[...]).reshape(tb, tt, d)
    yf_ref[...] = y
    yb_ref[...] = y.astype(BF16)

    @pl.when(ti == pl.num_programs(1) - 1)
    def _():
        for j in range(nj):
            gcol, vcol = state_cols(j)
            b_ref[:, :, gcol] = carry_ref[j, :, :, 0:FFN_COLS]
            b_ref[:, :, vcol] = carry_ref[j, :, :, FFN_COLS:]


def _ffn(x, buf, wu, cw, cb, wd, ln_g, ln_b, alpha, tb, tt):
    bsz, t, d = x.shape
    nj = wu.shape[0]
    d_ff = nj * FFN_COLS
    keep = F_CONV - 1
    row = pl.BlockSpec((tb, tt, d), lambda b, i: (b, i, 0))
    buf_spec = pl.BlockSpec((tb, keep, 2 * d_ff), lambda b, i: (b, 0, 0))
    const = lambda a: pl.BlockSpec(a.shape, lambda b, i: (0,) * a.ndim, pipeline_mode=pl.Buffered(1))
    lg = ln_g.reshape(1, d)
    lb = ln_b.reshape(1, d)
    return pl.pallas_call(
        functools.partial(_ffn_kernel, alpha=alpha, d_ff=d_ff),
        grid=(bsz // tb, t // tt),
        in_specs=[row, buf_spec, const(wu), const(cw), const(cb), const(wd), const(lg), const(lb)],
        out_specs=[row, row, buf_spec],
        out_shape=[jax.ShapeDtypeStruct((bsz, t, d), F32),
                   jax.ShapeDtypeStruct((bsz, t, d), BF16),
                   jax.ShapeDtypeStruct((bsz, keep, 2 * d_ff), F32)],
        scratch_shapes=[pltpu.VMEM((nj, tb, keep, 2 * FFN_COLS), F32),
                        pltpu.VMEM((tb, SUBLANES + tt, 2 * FFN_COLS), F32)],
        compiler_params=_cparams("parallel", "arbitrary"),
        name="conv_ffn",
    )(x, buf, wu, cw, cb, wd, lg, lb)


def _layer(xf, xb, state, p, prompt, alpha):
    k_cache, v_cache, s_hgrn, s_gdn, buf_gdn, buf_ffn = state
    bsz, t, d = xf.shape
    n = bsz * t
    x2 = xb.reshape(n, d)
    sa = _matmul(x2, p["wa"], p["wa"].shape[1]).reshape(bsz, t, -1)
    sb = _matmul(x2, p["wb"], 1024).reshape(bsz, t, -1)
    sc = _matmul(x2, p["wc"], p["wc"].shape[1]).reshape(bsz, t, -1)
    sg = _matmul(x2, p["wg"], 1024)
    oa, new_k, new_v = _swa(sa, k_cache, v_cache, p["sink"], prompt)
    ob, new_hgrn = _hgrn(sb, p["lb"], p["hgrn_w"], s_hgrn)
    oc, new_gdn, new_gconv = _gdn(sc, p["gdn_conv_w"], p["a_log"], p["dt_bias"], p["gdn_w"], s_gdn, buf_gdn)
    y = _merge(oa.reshape(n, -1), ob.reshape(n, -1), oc.reshape(n, -1), sg, xf.reshape(n, d),
               p["wba"], p["wbb"], p["wbc"], p["wout"], p["ln1_g"], p["ln1_b"], alpha)
    tb, tt = (1, 256) if prompt else (8, t)
    zf, zb, new_fconv = _ffn(y.reshape(bsz, t, d), buf_ffn, p["wu"], p["fcw"], p["fcb"], p["wd"],
                             p["ln2_g"], p["ln2_b"], alpha, tb, tt)
    return zf, zb, (new_k, new_v, new_hgrn, new_gdn, new_gconv, new_fconv)


def _pack_ffn(w_up, conv_w, conv_b, w_down):
    depth, d, two_ff = w_up.shape
    d_ff = two_ff // 2
    nj = d_ff // FFN_COLS

    def cols(a):
        lead = a.shape[1:-1]
        a = a.reshape((depth,) + lead + (2, nj, FFN_COLS))
        a = jnp.moveaxis(a, -2, 1)
        return a.reshape((depth, nj) + lead + (2 * FFN_COLS,))

    wu = cols(w_up).astype(BF16)
    cw = cols(conv_w).astype(F32)
    cb = cols(conv_b.reshape(depth, 1, two_ff)).astype(F32)
    wd = w_down.reshape(depth, nj, FFN_COLS, d).astype(BF16)
    return wu, cw, cb, wd


def kernel(x_prompt, x_sample, cache_swa_k, cache_swa_v, state_hgrn, state_gdn, state_gdn_conv, state_ffn_conv, ln_in_g, ln_in_b, w_in, attn_sinks, hgrn_lb_logits, hgrn_norm_w, gdn_conv_w, gdn_a_log, gdn_dt_bias, gdn_norm_w, w_branch, w_out, ln1_g, ln1_b, w_up, ffn_conv_w, ffn_conv_b, w_down, ln2_g, ln2_b):
    depth, d, _ = w_in.shape
    alpha = (2 * depth) ** 0.25
    lb_sm = jax.nn.softmax(hgrn_lb_logits.astype(F32), axis=0)
    lower_bounds = jnp.cumsum(lb_sm, axis=0) - lb_sm[0]

    o_a = A_WIDTH + 2 * A_KV_WIDTH
    o_b = o_a + 4 * B_WIDTH
    o_c = o_b + C_QKV + C_WIDTH
    o_s = o_c + 2 * C_HEADS
    wa = w_in[:, :, 0:o_a].astype(BF16)
    wb = w_in[:, :, o_a:o_b].astype(BF16)
    wc = jnp.concatenate([w_in[:, :, o_b:o_s], jnp.zeros((depth, d, LANES - 2 * C_HEADS), w_in.dtype)],
                         axis=-1).astype(BF16)
    wg = w_in[:, :, o_s:].astype(BF16)
    wba = w_branch[:, 0:A_WIDTH].astype(BF16)
    wbb = w_branch[:, A_WIDTH:A_WIDTH + B_WIDTH].astype(BF16)
    wbc = w_branch[:, A_WIDTH + B_WIDTH:].astype(BF16)
    wout = w_out.astype(BF16)
    wu, fcw, fcb, wd = _pack_ffn(w_up, ffn_conv_w, ffn_conv_b, w_down)

    hp_f, hp_b = _layer_norm(x_prompt, ln_in_g, ln_in_b)
    hs_f, hs_b = _layer_norm(x_sample, ln_in_g, ln_in_b)
    bp = x_prompt.shape[0]
    prompt_state = (None, None,
                    jnp.zeros((bp, B_HEADS, B_DIM, B_DIM), F32),
                    jnp.zeros((bp, C_HEADS, C_DIM, C_DIM), F32),
                    jnp.zeros((bp, C_CONV - 1, C_QKV), F32),
                    jnp.zeros((bp, F_CONV - 1, w_up.shape[-1]), F32))
    p_new, s_new = [], []
    for l in range(depth):
        p = dict(wa=wa[l], wb=wb[l], wc=wc[l], wg=wg[l], sink=attn_sinks[l].astype(F32), lb=lower_bounds[l],
                 hgrn_w=hgrn_norm_w[l], gdn_conv_w=gdn_conv_w[l], a_log=gdn_a_log[l], dt_bias=gdn_dt_bias[l],
                 gdn_w=gdn_norm_w[l], wba=wba[l], wbb=wbb[l], wbc=wbc[l], wout=wout[l],
                 ln1_g=ln1_g[l], ln1_b=ln1_b[l], wu=wu[l], fcw=fcw[l], fcb=fcb[l], wd=wd[l],
                 ln2_g=ln2_g[l], ln2_b=ln2_b[l])
        hp_f, hp_b, st_p = _layer(hp_f, hp_b, prompt_state, p, True, alpha)
        sample_state = (cache_swa_k[l], cache_swa_v[l], state_hgrn[l], state_gdn[l],
                        state_gdn_conv[l], state_ffn_conv[l])
        hs_f, hs_b, st_s = _layer(hs_f, hs_b, sample_state, p, False, alpha)
        p_new.append(st_p)
        s_new.append(st_s)

    def stack(states, i):
        return jnp.stack([st[i] for st in states])

    return (hp_f, hs_f) + tuple(stack(p_new, i) for i in range(6)) + tuple(stack(s_new, i) for i in range(6))
```

```python
import functools
import math

import jax
import jax.numpy as jnp
from jax import lax
from jax.experimental import pallas as pl
from jax.experimental.pallas import tpu as pltpu

F32 = jnp.float32
BF16 = jnp.bfloat16

CHUNK = 64
WINDOW = 128
A_HEADS = 8
A_KV_HEADS = 2
A_GROUP = A_HEADS // A_KV_HEADS
A_HEAD_DIM = 64
A_SCALE = A_HEAD_DIM ** -0.5
A_WIDTH = A_HEADS * A_HEAD_DIM
A_KV_WIDTH = A_KV_HEADS * A_HEAD_DIM
B_HEADS = 4
B_DIM = 128
C_HEADS = 4
C_DIM = 128
C_CONV = 4
F_CONV = 3
N_BRANCH = 3
B_WIDTH = B_HEADS * B_DIM
C_WIDTH = C_HEADS * C_DIM
C_QKV = 3 * C_WIDTH
LN_EPS = 1e-5
RMS_EPS = 1e-6
NEG_BIG = -1e30
F_MIN = 1e-30

LANES = 128
SUBLANES = 8
FFN_COLS = 256
VMEM_LIMIT = 48 * 1024 * 1024


def _cparams(*sem):
    return pltpu.CompilerParams(dimension_semantics=sem, vmem_limit_bytes=VMEM_LIMIT)


def _mm(a, b):
    return jnp.dot(a.astype(BF16), b.astype(BF16), preferred_element_type=F32)


def _mm_nt(a, b):
    return lax.dot_general(a.astype(BF16), b.astype(BF16), (((1,), (1,)), ((), ())),
                           preferred_element_type=F32)


def _mm_tn(a, b):
    return lax.dot_general(a.astype(BF16), b.astype(BF16), (((0,), (0,)), ((), ())),
                           preferred_element_type=F32)


def _mm_f32(a, b):
    return jnp.dot(a, b, preferred_element_type=F32, precision=lax.Precision.HIGHEST)


def _mm_nt_f32(a, b):
    return lax.dot_general(a, b, (((1,), (1,)), ((), ())), preferred_element_type=F32,
                           precision=lax.Precision.HIGHEST)


def _sigmoid(x):
    return 1.0 / (1.0 + jnp.exp(-x))


def _silu(x):
    return x * _sigmoid(x)


def _ln_rows(x, g, b):
    mu = jnp.mean(x, axis=-1, keepdims=True)
    xc = x - mu
    var = jnp.mean(xc * xc, axis=-1, keepdims=True)
    return xc * lax.rsqrt(var + LN_EPS) * g + b


def _rms_gated(o, w, z):
    n = o * lax.rsqrt(jnp.mean(o * o, axis=-1, keepdims=True) + RMS_EPS)
    return n * w * _silu(z)


def _tri(c, lower):
    r = lax.broadcasted_iota(jnp.int32, (c, c), 0)
    s = lax.broadcasted_iota(jnp.int32, (c, c), 1)
    return jnp.where(r >= s if lower else r <= s, 1.0, 0.0).astype(F32)


def _ln_kernel(x_ref, g_ref, b_ref, of_ref, ob_ref):
    y = _ln_rows(x_ref[...], g_ref[...], b_ref[...])
    of_ref[...] = y
    ob_ref[...] = y.astype(BF16)


def _layer_norm(x, g, b):
    bsz, t, d = x.shape
    n = bsz * t
    tm = min(512, n)
    of, ob = pl.pallas_call(
        _ln_kernel,
        grid=(n // tm,),
        in_specs=[pl.BlockSpec((tm, d), lambda i: (i, 0)),
                  pl.BlockSpec((1, d), lambda i: (0, 0)),
                  pl.BlockSpec((1, d), lambda i: (0, 0))],
        out_specs=[pl.BlockSpec((tm, d), lambda i: (i, 0)),
                   pl.BlockSpec((tm, d), lambda i: (i, 0))],
        out_shape=[jax.ShapeDtypeStruct((n, d), F32), jax.ShapeDtypeStruct((n, d), BF16)],
        compiler_params=_cparams("parallel"),
        name="input_ln",
    )(x.reshape(n, d), g.reshape(1, d), b.reshape(1, d))
    return of.reshape(bsz, t, d), ob.reshape(bsz, t, d)


def _mm_kernel(x_ref, w_ref, o_ref):
    o_ref[...] = jnp.dot(x_ref[...], w_ref[...], preferred_element_type=F32)


def _matmul(x, w, tn):
    n, k = x.shape
    dout = w.shape[1]
    tm = min(512, n)
    return pl.pallas_call(
        _mm_kernel,
        grid=(dout // tn, n // tm),
        in_specs=[pl.BlockSpec((tm, k), lambda j, i: (i, 0)),
                  pl.BlockSpec((k, tn), lambda j, i: (0, j))],
        out_specs=pl.BlockSpec((tm, tn), lambda j, i: (i, j)),
        out_shape=jax.ShapeDtypeStruct((n, dout), F32),
        compiler_params=_cparams("parallel", "parallel"),
        name="in_proj",
    )(x, w)


def _swa_kernel(q_ref, kvc_ref, kp_ref, vp_ref, sink_ref, o_ref, *cache_refs, prompt, qb):
    kvc = kvc_ref[...]
    kp = kp_ref[...]
    vp = vp_ref[...]
    nkeys = WINDOW + qb
    if prompt:
        row = lax.broadcasted_iota(jnp.int32, (qb, nkeys), 0)
        col = lax.broadcasted_iota(jnp.int32, (qb, nkeys), 1)
        behind = row // CHUNK - (col // CHUNK - WINDOW // CHUNK)
        first_valid = jnp.where(pl.program_id(1) == 0, WINDOW, 0)
        ok = (behind >= 0) & (behind <= WINDOW // CHUNK) & (col >= first_valid)
    kk, vv = [], []
    for kv in range(A_KV_HEADS):
        ks = slice(kv * A_HEAD_DIM, (kv + 1) * A_HEAD_DIM)
        vs = slice(A_KV_WIDTH + kv * A_HEAD_DIM, A_KV_WIDTH + (kv + 1) * A_HEAD_DIM)
        kk.append(jnp.concatenate([kp[:, ks], kvc[:, ks]], axis=0).astype(BF16))
        vv.append(jnp.concatenate([vp[:, ks], kvc[:, vs]], axis=0).astype(BF16))
    cols = [slice(h * A_HEAD_DIM, (h + 1) * A_HEAD_DIM) for h in range(A_HEADS)]
    scores = [_mm_nt(q_ref[:, cols[h]], kk[h // A_GROUP]) for h in range(A_HEADS)]
    es, dens = [], []
    for h in range(A_HEADS):
        s = scores[h] * A_SCALE
        if prompt:
            s = jnp.where(ok, s, NEG_BIG)
        sink = sink_ref[h]
        m = jnp.maximum(jnp.max(s, axis=-1, keepdims=True), sink)
        e = jnp.exp(s - m)
        es.append(e.astype(BF16))
        dens.append(jnp.sum(e, axis=-1, keepdims=True) + jnp.exp(sink - m))
    outs = [_mm(es[h], vv[h // A_GROUP]) for h in range(A_HEADS)]
    for h in range(A_HEADS):
        o_ref[:, cols[h]] = (outs[h] / dens[h]).astype(o_ref.dtype)
    if not prompt:
        nk_ref, nv_ref = cache_refs
        keep = WINDOW - qb
        nk_ref[0:keep, :] = kp[qb:, :]
        nk_ref[keep:, :] = kvc[:, 0:A_KV_WIDTH]
        nv_ref[0:keep, :] = vp[qb:, :]
        nv_ref[keep:, :] = kvc[:, A_KV_WIDTH:]


def _swa(sa, k_cache, v_cache, sink, prompt):
    bsz, t, _ = sa.shape
    qcol = 0
    kvcol = A_WIDTH // (2 * A_KV_WIDTH)
    if prompt:
        qb = WINDOW
        assert t % qb == 0
        kcol = A_WIDTH // A_KV_WIDTH
        kp_spec = pl.BlockSpec((None, WINDOW, A_KV_WIDTH), lambda b, i: (b, jnp.maximum(i - 1, 0), kcol))
        vp_spec = pl.BlockSpec((None, WINDOW, A_KV_WIDTH), lambda b, i: (b, jnp.maximum(i - 1, 0), kcol + 1))
        kp_arr, vp_arr = sa, sa
        out_shape = [jax.ShapeDtypeStruct((bsz, t, A_WIDTH), BF16)]
        out_specs = [pl.BlockSpec((None, qb, A_WIDTH), lambda b, i: (b, i, 0))]
    else:
        qb = t
        assert t <= WINDOW and t % (2 * SUBLANES) == 0 and k_cache.shape[1] == WINDOW
        kp_spec = pl.BlockSpec((None, WINDOW, A_KV_WIDTH), lambda b, i: (b, 0, 0))
        vp_spec = kp_spec
        kp_arr = k_cache.reshape(bsz, WINDOW, A_KV_WIDTH)
        vp_arr = v_cache.reshape(bsz, WINDOW, A_KV_WIDTH)
        out_shape = [jax.ShapeDtypeStruct((bsz, t, A_WIDTH), BF16),
                     jax.ShapeDtypeStruct((bsz, WINDOW, A_KV_WIDTH), F32),
                     jax.ShapeDtypeStruct((bsz, WINDOW, A_KV_WIDTH), F32)]
        cache_spec = pl.BlockSpec((None, WINDOW, A_KV_WIDTH), lambda b, i: (b, 0, 0))
        out_specs = [pl.BlockSpec((None, qb, A_WIDTH), lambda b, i: (b, i, 0)), cache_spec, cache_spec]
    outs = pl.pallas_call(
        functools.partial(_swa_kernel, prompt=prompt, qb=qb),
        grid=(bsz, t // qb),
        in_specs=[pl.BlockSpec((None, qb, A_WIDTH), lambda b, i: (b, i, qcol)),
                  pl.BlockSpec((None, qb, 2 * A_KV_WIDTH), lambda b, i: (b, i, kvcol)),
                  kp_spec, vp_spec,
                  pl.BlockSpec(memory_space=pltpu.SMEM)],
        out_specs=out_specs,
        out_shape=out_shape,
        compiler_params=_cparams("parallel", "parallel"),
        name="swa",
    )(sa, sa, kp_arr, vp_arr, sink)
    if prompt:
        new_k = sa[:, t - WINDOW:, A_WIDTH:A_WIDTH + A_KV_WIDTH]
        new_v = sa[:, t - WINDOW:, A_WIDTH + A_KV_WIDTH:]
    else:
        new_k, new_v = outs[1], outs[2]
    shape = (bsz, WINDOW, A_KV_HEADS, A_HEAD_DIM)
    return outs[0], new_k.reshape(shape), new_v.reshape(shape)


def _hgrn_kernel(x_ref, lb_ref, w_ref, s0_ref, o_ref, s_ref, st_ref, *, c):
    tb, tt, _ = x_ref.shape
    ti = pl.program_id(1)

    @pl.when(ti == 0)
    def _():
        for b in range(tb):
            for h in range(B_HEADS):
                st_ref[b, h] = s0_ref[b, h].T

    pos = lax.broadcasted_iota(jnp.int32, (tt, B_DIM), 0)
    r = lax.broadcasted_iota(jnp.int32, (c, c), 0)
    s = lax.broadcasted_iota(jnp.int32, (c, c), 1)
    x = r ^ s
    for b in range(tb):
        for h in range(B_HEADS):
            cs = slice(h * B_DIM, (h + 1) * B_DIM)
            qp = x_ref[b, :, cs]
            fp = x_ref[b, :, B_WIDTH + h * B_DIM:B_WIDTH + (h + 1) * B_DIM]
            v = x_ref[b, :, 2 * B_WIDTH + h * B_DIM:2 * B_WIDTH + (h + 1) * B_DIM]
            gp = x_ref[b, :, 3 * B_WIDTH + h * B_DIM:3 * B_WIDTH + (h + 1) * B_DIM]
            lb = lb_ref[:, cs]
            sig = _sigmoid(fp)
            f = jnp.maximum(lb + (1.0 - lb) * sig, F_MIN)
            k = (1.0 - lb) * (1.0 - sig)
            q = _silu(qp)
            total = f
            prefix = f
            suffix = jnp.ones_like(f)
            att = [jnp.where(r == s, jnp.sum((q * k)[j * c:(j + 1) * c], axis=-1, keepdims=True), 0.0)
                   for j in range(tt // c)]
            hh = 1
            while hh < c:
                qh = (q * prefix).astype(BF16)
                kh = (k * suffix).astype(BF16)
                level = (r > s) & (x >= hh) & (x < 2 * hh)
                for j in range(tt // c):
                    rows = slice(j * c, (j + 1) * c)
                    att[j] = jnp.where(level, _mm_nt(qh[rows], kh[rows]), att[j])
                upper = (pos % (2 * hh)) >= hh
                below = pltpu.roll(total, hh, axis=0)
                above = pltpu.roll(total, tt - hh, axis=0)
                prefix = prefix * jnp.where(upper, below, 1.0)
                suffix = suffix * jnp.where(upper, 1.0, above)
                total = total * jnp.where(upper, below, above)
                hh *= 2
            qc = (q * prefix).astype(BF16)
            kc = (k * suffix).astype(BF16)
            for j in range(tt // c):
                rows = slice(j * c, (j + 1) * c)
                st = st_ref[b, h]
                o = _mm_nt(qc[rows], st) + _mm(att[j], v[rows])
                st_ref[b, h] = st * total[(j + 1) * c - 1:(j + 1) * c, :] + _mm_tn(v[rows], kc[rows])
                o_ref[b, rows, cs] = _rms_gated(o, w_ref[...], gp[rows]).astype(o_ref.dtype)

    @pl.when(ti == pl.num_programs(1) - 1)
    def _():
        for b in range(tb):
            for h in range(B_HEADS):
                s_ref[b, h] = st_ref[b, h].T


def _hgrn(sb, lb, norm_w, state, tb, tt):
    bsz, t, _ = sb.shape
    c = min(CHUNK, t)
    assert tt % c == 0 and t % tt == 0 and bsz % tb == 0
    st_spec = pl.BlockSpec((tb, B_HEADS, B_DIM, B_DIM), lambda b, i: (b, 0, 0, 0))
    return pl.pallas_call(
        functools.partial(_hgrn_kernel, c=c),
        grid=(bsz // tb, t // tt),
        in_specs=[pl.BlockSpec((tb, tt, 4 * B_WIDTH), lambda b, i: (b, i, 0)),
                  pl.BlockSpec((1, B_WIDTH), lambda b, i: (0, 0)),
                  pl.BlockSpec((1, B_DIM), lambda b, i: (0, 0)),
                  st_spec],
        out_specs=[pl.BlockSpec((tb, tt, B_WIDTH), lambda b, i: (b, i, 0)), st_spec],
        out_shape=[jax.ShapeDtypeStruct((bsz, t, B_WIDTH), BF16),
                   jax.ShapeDtypeStruct((bsz, B_HEADS, B_DIM, B_DIM), F32)],
        scratch_shapes=[pltpu.VMEM((tb, B_HEADS, B_DIM, B_DIM), F32)],
        compiler_params=_cparams("parallel", "arbitrary"),
        name="hgrn",
    )(sb, lb.reshape(1, B_WIDTH), norm_w.reshape(1, B_DIM), state)


def _gdn_prep_kernel(x_ref, sm_ref, prev_ref, b0_ref, cw_ref, alog_ref, dtb_ref,
                     sv_ref, sk_ref, qk_ref, qt_ref, kt_ref, ec_ref, cbuf_ref, *, c):
    tb, tt, _ = x_ref.shape
    pad = SUBLANES
    keep = C_CONV - 1
    ti = pl.program_id(1)

    @pl.when(ti == 0)
    def _():
        cbuf_ref[:, pad - keep:pad, :] = b0_ref[...]

    @pl.when(ti > 0)
    def _():
        cbuf_ref[:, pad - keep:pad, :] = prev_ref[:, pad - keep:pad, :]

    cbuf_ref[:, pad:pad + tt, :] = x_ref[...]
    u = cbuf_ref[:, pad - keep:pad - keep + tt, :] * cw_ref[0:1, :]
    for j in range(1, C_CONV):
        u = u + cbuf_ref[:, pad - keep + j:pad - keep + j + tt, :] * cw_ref[j:j + 1, :]
    u = _silu(u)

    small = sm_ref[...]
    pre = small + dtb_ref[...]
    softplus = jnp.maximum(pre, 0.0) + jnp.log(1.0 + jnp.exp(-jnp.abs(pre)))
    g_all = -jnp.exp(alog_ref[...]) * softplus
    beta_all = _sigmoid(small)
    tr = lax.broadcasted_iota(jnp.int32, (tt, tt), 0)
    ts = lax.broadcasted_iota(jnp.int32, (tt, tt), 1)
    chunk_tri = jnp.where((tr >= ts) & (tr // c == ts // c), 1.0, 0.0).astype(F32)
    sel_r = lax.broadcasted_iota(jnp.int32, (SUBLANES, LANES), 0)
    sel_c = lax.broadcasted_iota(jnp.int32, (SUBLANES, LANES), 1)
    sel = jnp.where(sel_r == sel_c, 1.0, 0.0).astype(F32)
    r = lax.broadcasted_iota(jnp.int32, (c, c), 0)
    s = lax.broadcasted_iota(jnp.int32, (c, c), 1)
    incl = r >= s
    units = []
    for b in range(tb):
        cum_all = _mm_f32(chunk_tri, g_all[b])
        cum_rows = _mm_nt_f32(sel, cum_all)
        ecum_all = jnp.exp(cum_all)
        ec_ref[b] = ecum_all
        for h in range(C_HEADS):
            cs = slice(h * C_DIM, (h + 1) * C_DIM)
            uq = u[b, :, cs]
            uk = u[b, :, C_WIDTH + h * C_DIM:C_WIDTH + (h + 1) * C_DIM]
            v = u[b, :, 2 * C_WIDTH + h * C_DIM:2 * C_WIDTH + (h + 1) * C_DIM]
            q = uq * lax.rsqrt(jnp.sum(uq * uq, axis=-1, keepdims=True) + 1e-6) * (C_DIM ** -0.5)
            k = uk * lax.rsqrt(jnp.sum(uk * uk, axis=-1, keepdims=True) + 1e-6)
            beta = beta_all[b, :, h:h + 1]
            cum = cum_all[:, C_HEADS + h:C_HEADS + h + 1]
            ecum = ecum_all[:, C_HEADS + h:C_HEADS + h + 1]
            qt_ref[b, :, cs] = (q * ecum).astype(BF16)
            rhs = jnp.concatenate([v * beta, k * (beta * ecum)], axis=-1)
            for j in range(tt // c):
                rows = slice(j * c, (j + 1) * c)
                cum_j = cum[rows]
                cum_row = cum_rows[C_HEADS + h:C_HEADS + h + 1, rows]
                dec = jnp.where(incl, jnp.exp(jnp.where(incl, cum_j - cum_row, 0.0)), 0.0)
                k_j = k[rows]
                kb = k_j.astype(BF16)
                last = cum_j[c - 1:c, :]
                kt_ref[b, rows, cs] = (k_j * jnp.exp(last - cum_j)).astype(BF16)
                units.append(dict(b=b, rows=rows, cs=cs, qcols=slice(h * c, (h + 1) * c), dec=dec, kb=kb,
                                  qb=q[rows].astype(BF16), beta=beta[rows], rhs=rhs[rows]))
    for un in units:
        un["kk"] = _mm_nt(un["kb"], un["kb"])
        un["qk"] = _mm_nt(un["qb"], un["kb"])
    for un in units:
        qk_ref[un["b"], un["rows"], un["qcols"]] = (un["qk"] * un["dec"]).astype(BF16)
        n = jnp.where(r > s, un["beta"] * un["kk"] * un["dec"], 0.0)
        un["p"] = n
        un["corr"] = -n
    for _ in range(int(math.log2(c)) - 1):
        for un in units:
            un["p"] = _mm(un["p"], un["p"])
        for un in units:
            un["corr"] = un["corr"] + un["p"] + _mm(un["corr"], un["p"])
    for un in units:
        un["sol"] = un["rhs"] + _mm(un["corr"], un["rhs"])
    for un in units:
        sv_ref[un["b"], un["rows"], un["cs"]] = un["sol"][:, 0:C_DIM]
        sk_ref[un["b"], un["rows"], un["cs"]] = un["sol"][:, C_DIM:].astype(BF16)


def _gdn_scan_kernel(sv_ref, sk_ref, qk_ref, qt_ref, kt_ref, ec_ref, z_ref, w_ref, s0_ref,
                     o_ref, s_ref, st_ref, *, c):
    tb, tt, _ = sv_ref.shape
    ti = pl.program_id(1)

    @pl.when(ti == 0)
    def _():
        st_ref[...] = s0_ref[...]

    heads = [(b, h, slice(h * C_DIM, (h + 1) * C_DIM)) for b in range(tb) for h in range(C_HEADS)]
    for j in range(tt // c):
        rows = slice(j * c, (j + 1) * c)
        st = [st_ref[b, h] for b, h, _ in heads]
        stb = [a.astype(BF16) for a in st]
        sks = [_mm(sk_ref[b, rows, cs], stb[i]) for i, (b, h, cs) in enumerate(heads)]
        qs = [_mm(qt_ref[b, rows, cs], stb[i]) for i, (b, h, cs) in enumerate(heads)]
        wb = [(sv_ref[b, rows, cs] - sks[i]).astype(BF16) for i, (b, h, cs) in enumerate(heads)]
        upd = [_mm_tn(kt_ref[b, rows, cs], wb[i]) for i, (b, h, cs) in enumerate(heads)]
        qkw = [_mm(qk_ref[b, rows, h * c:(h + 1) * c], wb[i]) for i, (b, h, cs) in enumerate(heads)]
        for i, (b, h, cs) in enumerate(heads):
            e_last = ec_ref[b, (j + 1) * c - 1:(j + 1) * c, C_HEADS + h:C_HEADS + h + 1]
            st_ref[b, h] = e_last * st[i] + upd[i]
            o_ref[b, rows, cs] = _rms_gated(qs[i] + qkw[i], w_ref[...], z_ref[b, rows, cs]).astype(o_ref.dtype)

    @pl.when(ti == pl.num_programs(1) - 1)
    def _():
        s_ref[...] = st_ref[...]


def _gdn(sc, conv_w, a_log, dt_bias, norm_w, state, buf, tb_prep, tb_scan, tt):
    bsz, t, _ = sc.shape
    c = min(CHUNK, t)
    keep = C_CONV - 1
    assert t >= keep and tt % c == 0 and t % tt == 0
    pad_vec = lambda a: jnp.zeros((1, LANES), F32).at[0, C_HEADS:2 * C_HEADS].set(a.astype(F32))
    const = lambda shape: pl.BlockSpec(shape, lambda b, i: (0,) * len(shape))
    tile = lambda tb, w, col=0: pl.BlockSpec((tb, tt, w), lambda b, i: (b, i, col))
    shape = lambda w, dt: jax.ShapeDtypeStruct((bsz, t, w), dt)
    small_col = (C_QKV + C_WIDTH) // LANES
    prev_spec = pl.BlockSpec((tb_prep, SUBLANES, C_QKV),
                             lambda b, i: (b, jnp.maximum(i * (tt // SUBLANES) - 1, 0), 0))
    sv, sk, qk, qt, kt, ec = pl.pallas_call(
        functools.partial(_gdn_prep_kernel, c=c),
        grid=(bsz // tb_prep, t // tt),
        in_specs=[tile(tb_prep, C_QKV), tile(tb_prep, LANES, small_col), prev_spec,
                  pl.BlockSpec((tb_prep, keep, C_QKV), lambda b, i: (b, 0, 0)),
                  const((C_CONV, C_QKV)), const((1, LANES)), const((1, LANES))],
        out_specs=[tile(tb_prep, C_WIDTH), tile(tb_prep, C_WIDTH), tile(tb_prep, C_HEADS * c),
                   tile(tb_prep, C_WIDTH), tile(tb_prep, C_WIDTH), tile(tb_prep, LANES)],
        out_shape=[shape(C_WIDTH, F32), shape(C_WIDTH, BF16), shape(C_HEADS * c, BF16),
                   shape(C_WIDTH, BF16), shape(C_WIDTH, BF16), shape(LANES, F32)],
        scratch_shapes=[pltpu.VMEM((tb_prep, SUBLANES + tt, C_QKV), F32)],
        compiler_params=_cparams("parallel", "parallel"),
        name="gdn_prep",
    )(sc, sc, sc, buf, conv_w, pad_vec(a_log), pad_vec(dt_bias))
    st_spec = pl.BlockSpec((tb_scan, C_HEADS, C_DIM, C_DIM), lambda b, i: (b, 0, 0, 0))
    o, new_state = pl.pallas_call(
        functools.partial(_gdn_scan_kernel, c=c),
        grid=(bsz // tb_scan, t // tt),
        in_specs=[tile(tb_scan, C_WIDTH), tile(tb_scan, C_WIDTH), tile(tb_scan, C_HEADS * c),
                  tile(tb_scan, C_WIDTH), tile(tb_scan, C_WIDTH), tile(tb_scan, LANES),
                  tile(tb_scan, C_WIDTH, C_QKV // C_WIDTH), const((1, C_DIM)), st_spec],
        out_specs=[tile(tb_scan, C_WIDTH), st_spec],
        out_shape=[shape(C_WIDTH, BF16), jax.ShapeDtypeStruct((bsz, C_HEADS, C_DIM, C_DIM), F32)],
        scratch_shapes=[pltpu.VMEM((tb_scan, C_HEADS, C_DIM, C_DIM), F32)],
        compiler_params=_cparams("parallel", "arbitrary"),
        name="gdn_scan",
    )(sv, sk, qk, qt, kt, ec, sc, norm_w.reshape(1, C_DIM), state)
    return o, new_state, sc[:, t - keep:, 0:C_QKV]


def _merge_kernel(oa_ref, ob_ref, oc_ref, g_ref, x_ref, wa_ref, wb_ref, wc_ref, wo_ref,
                  lg_ref, lb_ref, y_ref, *, alpha):
    d = x_ref.shape[-1]
    mix = _sigmoid(g_ref[:, 0:d]) * _mm(oa_ref[...], wa_ref[...])
    mix = mix + _sigmoid(g_ref[:, d:2 * d]) * _mm(ob_ref[...], wb_ref[...])
    mix = mix + _sigmoid(g_ref[:, 2 * d:3 * d]) * _mm(oc_ref[...], wc_ref[...])
    y_ref[...] = _ln_rows(alpha * x_ref[...] + _mm(mix, wo_ref[...]), lg_ref[...], lb_ref[...])


def _merge(oa, ob, oc, gates, x, wba, wbb, wbc, wout, ln_g, ln_b, alpha):
    n, d = x.shape
    tm = min(256, n)
    row = lambda w: pl.BlockSpec((tm, w), lambda i: (i, 0))
    const = lambda a: pl.BlockSpec(a.shape, lambda i: (0, 0))
    return pl.pallas_call(
        functools.partial(_merge_kernel, alpha=alpha),
        grid=(n // tm,),
        in_specs=[row(A_WIDTH), row(B_WIDTH), row(C_WIDTH), row(N_BRANCH * d), row(d),
                  const(wba), const(wbb), const(wbc), const(wout),
                  pl.BlockSpec((1, d), lambda i: (0, 0)), pl.BlockSpec((1, d), lambda i: (0, 0))],
        out_specs=row(d),
        out_shape=jax.ShapeDtypeStruct((n, d), F32),
        compiler_params=_cparams("parallel"),
        name="merge",
    )(oa, ob, oc, gates, x, wba, wbb, wbc, wout, ln_g.reshape(1, d), ln_b.reshape(1, d))


def _ffn_kernel(x_ref, b0_ref, wu_ref, cw_ref, cb_ref, wd_ref, lg_ref, lb_ref,
                yf_ref, yb_ref, b_ref, carry_ref, win_ref, *, alpha, d_ff):
    tb, tt, d = x_ref.shape
    m = tb * tt
    nj = d_ff // FFN_COLS
    keep = F_CONV - 1
    pad = SUBLANES
    ti = pl.program_id(1)

    def state_cols(j):
        return (slice(j * FFN_COLS, (j + 1) * FFN_COLS), slice(d_ff + j * FFN_COLS, d_ff + (j + 1) * FFN_COLS))

    @pl.when(ti == 0)
    def _():
        for j in range(nj):
            gcol, vcol = state_cols(j)
            carry_ref[j, :, :, 0:FFN_COLS] = b0_ref[:, :, gcol]
            carry_ref[j, :, :, FFN_COLS:] = b0_ref[:, :, vcol]

    x = x_ref[...].reshape(m, d)
    xb = x.astype(BF16)
    acc = jnp.zeros((m, d), F32)
    for j in range(nj):
        up = jnp.dot(xb, wu_ref[j], preferred_element_type=F32)
        win_ref[:, pad - keep:pad, :] = carry_ref[j]
        win_ref[:, pad:pad + tt, :] = up.reshape(tb, tt, 2 * FFN_COLS)
        conv = win_ref[:, pad - keep:pad - keep + tt, :] * cw_ref[j, 0:1, :]
        for i in range(1, F_CONV):
            conv = conv + win_ref[:, pad - keep + i:pad - keep + i + tt, :] * cw_ref[j, i:i + 1, :]
        conv = conv + cb_ref[j]
        carry_ref[j] = win_ref[:, pad + tt - keep:pad + tt, :]
        act = (_silu(conv[:, :, 0:FFN_COLS]) * conv[:, :, FFN_COLS:]).reshape(m, FFN_COLS)
        acc = acc + jnp.dot(act.astype(BF16), wd_ref[j], preferred_element_type=F32)
    y = _ln_rows(alpha * x + acc, lg_ref[...], lb_ref[...]).reshape(tb, tt, d)
    yf_ref[...] = y
    yb_ref[...] = y.astype(BF16)

    @pl.when(ti == pl.num_programs(1) - 1)
    def _():
        for j in range(nj):
            gcol, vcol = state_cols(j)
            b_ref[:, :, gcol] = carry_ref[j, :, :, 0:FFN_COLS]
            b_ref[:, :, vcol] = carry_ref[j, :, :, FFN_COLS:]


def _ffn(x, buf, wu, cw, cb, wd, ln_g, ln_b, alpha, tb, tt):
    bsz, t, d = x.shape
    nj = wu.shape[0]
    d_ff = nj * FFN_COLS
    keep = F_CONV - 1
    row = pl.BlockSpec((tb, tt, d), lambda b, i: (b, i, 0))
    buf_spec = pl.BlockSpec((tb, keep, 2 * d_ff), lambda b, i: (b, 0, 0))
    const = lambda a: pl.BlockSpec(a.shape, lambda b, i: (0,) * a.ndim, pipeline_mode=pl.Buffered(1))
    lg = ln_g.reshape(1, d)
    lb = ln_b.reshape(1, d)
    return pl.pallas_call(
        functools.partial(_ffn_kernel, alpha=alpha, d_ff=d_ff),
        grid=(bsz // tb, t // tt),
        in_specs=[row, buf_spec, const(wu), const(cw), const(cb), const(wd), const(lg), const(lb)],
        out_specs=[row, row, buf_spec],
        out_shape=[jax.ShapeDtypeStruct((bsz, t, d), F32),
                   jax.ShapeDtypeStruct((bsz, t, d), BF16),
                   jax.ShapeDtypeStruct((bsz, keep, 2 * d_ff), F32)],
        scratch_shapes=[pltpu.VMEM((nj, tb, keep, 2 * FFN_COLS), F32),
                        pltpu.VMEM((tb, SUBLANES + tt, 2 * FFN_COLS), F32)],
        compiler_params=_cparams("parallel", "arbitrary"),
        name="conv_ffn",
    )(x, buf, wu, cw, cb, wd, lg, lb)


def _layer(xf, xb, state, p, prompt, alpha):
    k_cache, v_cache, s_hgrn, s_gdn, buf_gdn, buf_ffn = state
    bsz, t, d = xf.shape
    n = bsz * t
    x2 = xb.reshape(n, d)
    sa = _matmul(x2, p["wa"], p["wa"].shape[1]).reshape(bsz, t, -1)
    sb = _matmul(x2, p["wb"], 1024).reshape(bsz, t, -1)
    sc = _matmul(x2, p["wc"], p["wc"].shape[1]).reshape(bsz, t, -1)
    sg = _matmul(x2, p["wg"], 1024)
    oa, new_k, new_v = _swa(sa, k_cache, v_cache, p["sink"], prompt)
    ob, new_hgrn = _hgrn(sb, p["lb"], p["hgrn_w"], s_hgrn, *((1, 256) if prompt else (8, t)))
    gdn_tiles = (1, bsz, 256) if prompt else (8, 8, t)
    oc, new_gdn, new_gconv = _gdn(sc, p["gdn_conv_w"], p["a_log"], p["dt_bias"], p["gdn_w"], s_gdn, buf_gdn,
                                  *gdn_tiles)
    y = _merge(oa.reshape(n, -1), ob.reshape(n, -1), oc.reshape(n, -1), sg, xf.reshape(n, d),
               p["wba"], p["wbb"], p["wbc"], p["wout"], p["ln1_g"], p["ln1_b"], alpha)
    tb, tt = (1, 256) if prompt else (8, t)
    zf, zb, new_fconv = _ffn(y.reshape(bsz, t, d), buf_ffn, p["wu"], p["fcw"], p["fcb"], p["wd"],
                             p["ln2_g"], p["ln2_b"], alpha, tb, tt)
    return zf, zb, (new_k, new_v, new_hgrn, new_gdn, new_gconv, new_fconv)


def _pack_ffn(w_up, conv_w, conv_b, w_down):
    depth, d, two_ff = w_up.shape
    d_ff = two_ff // 2
    nj = d_ff // FFN_COLS

    def cols(a):
        lead = a.shape[1:-1]
        a = a.reshape((depth,) + lead + (2, nj, FFN_COLS))
        a = jnp.moveaxis(a, -2, 1)
        return a.reshape((depth, nj) + lead + (2 * FFN_COLS,))

    wu = cols(w_up).astype(BF16)
    cw = cols(conv_w).astype(F32)
    cb = cols(conv_b.reshape(depth, 1, two_ff)).astype(F32)
    wd = w_down.reshape(depth, nj, FFN_COLS, d).astype(BF16)
    return wu, cw, cb, wd


def kernel(x_prompt, x_sample, cache_swa_k, cache_swa_v, state_hgrn, state_gdn, state_gdn_conv, state_ffn_conv, ln_in_g, ln_in_b, w_in, attn_sinks, hgrn_lb_logits, hgrn_norm_w, gdn_conv_w, gdn_a_log, gdn_dt_bias, gdn_norm_w, w_branch, w_out, ln1_g, ln1_b, w_up, ffn_conv_w, ffn_conv_b, w_down, ln2_g, ln2_b):
    depth, d, _ = w_in.shape
    alpha = (2 * depth) ** 0.25
    lb_sm = jax.nn.softmax(hgrn_lb_logits.astype(F32), axis=0)
    lower_bounds = jnp.cumsum(lb_sm, axis=0) - lb_sm[0]

    o_a = A_WIDTH + 2 * A_KV_WIDTH
    o_b = o_a + 4 * B_WIDTH
    o_c = o_b + C_QKV + C_WIDTH
    o_s = o_c + 2 * C_HEADS
    wa = w_in[:, :, 0:o_a].astype(BF16)
    wb = w_in[:, :, o_a:o_b].astype(BF16)
    wc = jnp.concatenate([w_in[:, :, o_b:o_s], jnp.zeros((depth, d, LANES - 2 * C_HEADS), w_in.dtype)],
                         axis=-1).astype(BF16)
    wg = w_in[:, :, o_s:].astype(BF16)
    wba = w_branch[:, 0:A_WIDTH].astype(BF16)
    wbb = w_branch[:, A_WIDTH:A_WIDTH + B_WIDTH].astype(BF16)
    wbc = w_branch[:, A_WIDTH + B_WIDTH:].astype(BF16)
    wout = w_out.astype(BF16)
    wu, fcw, fcb, wd = _pack_ffn(w_up, ffn_conv_w, ffn_conv_b, w_down)

    hp_f, hp_b = _layer_norm(x_prompt, ln_in_g, ln_in_b)
    hs_f, hs_b = _layer_norm(x_sample, ln_in_g, ln_in_b)
    bp = x_prompt.shape[0]
    prompt_state = (None, None,
                    jnp.zeros((bp, B_HEADS, B_DIM, B_DIM), F32),
                    jnp.zeros((bp, C_HEADS, C_DIM, C_DIM), F32),
                    jnp.zeros((bp, C_CONV - 1, C_QKV), F32),
                    jnp.zeros((bp, F_CONV - 1, w_up.shape[-1]), F32))
    p_new, s_new = [], []
    for l in range(depth):
        p = dict(wa=wa[l], wb=wb[l], wc=wc[l], wg=wg[l], sink=attn_sinks[l].astype(F32), lb=lower_bounds[l],
                 hgrn_w=hgrn_norm_w[l], gdn_conv_w=gdn_conv_w[l], a_log=gdn_a_log[l], dt_bias=gdn_dt_bias[l],
                 gdn_w=gdn_norm_w[l], wba=wba[l], wbb=wbb[l], wbc=wbc[l], wout=wout[l],
                 ln1_g=ln1_g[l], ln1_b=ln1_b[l], wu=wu[l], fcw=fcw[l], fcb=fcb[l], wd=wd[l],
                 ln2_g=ln2_g[l], ln2_b=ln2_b[l])
        hp_f, hp_b, st_p = _layer(hp_f, hp_b, prompt_state, p, True, alpha)
        sample_state = (cache_swa_k[l], cache_swa_v[l], state_hgrn[l], state_gdn[l],
                        state_gdn_conv[l], state_ffn_conv[l])
        hs_f, hs_b, st_s = _layer(hs_f, hs_b, sample_state, p, False, alpha)
        p_new.append(st_p)
        s_new.append(st_s)

    def stack(states, i):
        return jnp.stack([st[i] for st in states])

    return (hp_f, hs_f) + tuple(stack(p_new, i) for i in range(6)) + tuple(stack(s_new, i) for i in range(6))
```

```python
import functools
import math

import jax
import jax.numpy as jnp
from jax import lax
from jax.experimental import pallas as pl
from jax.experimental.pallas import tpu as pltpu

F32 = jnp.float32
BF16 = jnp.bfloat16

CHUNK = 64
WINDOW = 128
A_HEADS = 8
A_KV_HEADS = 2
A_GROUP = A_HEADS // A_KV_HEADS
A_HEAD_DIM = 64
A_SCALE = A_HEAD_DIM ** -0.5
A_WIDTH = A_HEADS * A_HEAD_DIM
A_KV_WIDTH = A_KV_HEADS * A_HEAD_DIM
B_HEADS = 4
B_DIM = 128
C_HEADS = 4
C_DIM = 128
C_CONV = 4
F_CONV = 3
N_BRANCH = 3
B_WIDTH = B_HEADS * B_DIM
C_WIDTH = C_HEADS * C_DIM
C_QKV = 3 * C_WIDTH
LN_EPS = 1e-5
RMS_EPS = 1e-6
NEG_BIG = -1e30
F_MIN = 1e-30

LANES = 128
SUBLANES = 8
FFN_COLS = 256
VMEM_LIMIT = 48 * 1024 * 1024


def _cparams(*sem):
    return pltpu.CompilerParams(dimension_semantics=sem, vmem_limit_bytes=VMEM_LIMIT)


def _mm(a, b):
    return jnp.dot(a.astype(BF16), b.astype(BF16), preferred_element_type=F32)


def _mm_nt(a, b):
    return lax.dot_general(a.astype(BF16), b.astype(BF16), (((1,), (1,)), ((), ())),
                           preferred_element_type=F32)


def _mm_tn(a, b):
    return lax.dot_general(a.astype(BF16), b.astype(BF16), (((0,), (0,)), ((), ())),
                           preferred_element_type=F32)


def _mm_f32(a, b):
    return jnp.dot(a, b, preferred_element_type=F32, precision=lax.Precision.HIGHEST)


def _mm_nt_f32(a, b):
    return lax.dot_general(a, b, (((1,), (1,)), ((), ())), preferred_element_type=F32,
                           precision=lax.Precision.HIGHEST)


def _sigmoid(x):
    return 1.0 / (1.0 + jnp.exp(-x))


def _silu(x):
    return x * _sigmoid(x)


def _ln_rows(x, g, b):
    mu = jnp.mean(x, axis=-1, keepdims=True)
    xc = x - mu
    var = jnp.mean(xc * xc, axis=-1, keepdims=True)
    return xc * lax.rsqrt(var + LN_EPS) * g + b


def _rms_gated(o, w, z):
    n = o * lax.rsqrt(jnp.mean(o * o, axis=-1, keepdims=True) + RMS_EPS)
    return n * w * _silu(z)


def _tri(c, lower):
    r = lax.broadcasted_iota(jnp.int32, (c, c), 0)
    s = lax.broadcasted_iota(jnp.int32, (c, c), 1)
    return jnp.where(r >= s if lower else r <= s, 1.0, 0.0).astype(F32)


def _ln_kernel(x_ref, g_ref, b_ref, of_ref, ob_ref):
    y = _ln_rows(x_ref[...], g_ref[...], b_ref[...])
    of_ref[...] = y
    ob_ref[...] = y.astype(BF16)


def _layer_norm(x, g, b):
    bsz, t, d = x.shape
    n = bsz * t
    tm = min(512, n)
    of, ob = pl.pallas_call(
        _ln_kernel,
        grid=(n // tm,),
        in_specs=[pl.BlockSpec((tm, d), lambda i: (i, 0)),
                  pl.BlockSpec((1, d), lambda i: (0, 0)),
                  pl.BlockSpec((1, d), lambda i: (0, 0))],
        out_specs=[pl.BlockSpec((tm, d), lambda i: (i, 0)),
                   pl.BlockSpec((tm, d), lambda i: (i, 0))],
        out_shape=[jax.ShapeDtypeStruct((n, d), F32), jax.ShapeDtypeStruct((n, d), BF16)],
        compiler_params=_cparams("parallel"),
        name="input_ln",
    )(x.reshape(n, d), g.reshape(1, d), b.reshape(1, d))
    return of.reshape(bsz, t, d), ob.reshape(bsz, t, d)


def _in_proj_kernel(x_ref, wq_ref, wkv_ref, wb_ref, wc_ref, wg_ref, q_ref, kv_ref, b_ref, c_ref, g_ref):
    x = x_ref[...]
    dot = lambda w_ref: jnp.dot(x, w_ref[...], preferred_element_type=F32)
    q_ref[...] = dot(wq_ref).astype(BF16)
    kv_ref[...] = dot(wkv_ref)
    b_ref[...] = dot(wb_ref)
    c_ref[...] = dot(wc_ref)
    g_ref[...] = _sigmoid(dot(wg_ref)).astype(BF16)


def _in_proj(x, wq, wkv, wb, wc, wg):
    n, k = x.shape
    tm = min(256, n)
    ws = (wq, wkv, wb, wc, wg)
    dts = (BF16, F32, F32, F32, BF16)
    return pl.pallas_call(
        _in_proj_kernel,
        grid=(n // tm,),
        in_specs=[pl.BlockSpec((tm, k), lambda i: (i, 0))] +
                 [pl.BlockSpec(w.shape, lambda i: (0, 0), pipeline_mode=pl.Buffered(1)) for w in ws],
        out_specs=[pl.BlockSpec((tm, w.shape[1]), lambda i: (i, 0)) for w in ws],
        out_shape=[jax.ShapeDtypeStruct((n, w.shape[1]), dt) for w, dt in zip(ws, dts)],
        compiler_params=_cparams("parallel"),
        name="in_proj",
    )(x, *ws)


def _swa_kernel(q_ref, kvc_ref, kp_ref, vp_ref, sink_ref, o_ref, *cache_refs, prompt, qb):
    kvc = kvc_ref[...]
    kp = kp_ref[...]
    vp = vp_ref[...]
    nkeys = WINDOW + qb
    if prompt:
        row = lax.broadcasted_iota(jnp.int32, (qb, nkeys), 0)
        col = lax.broadcasted_iota(jnp.int32, (qb, nkeys), 1)
        behind = row // CHUNK - (col // CHUNK - WINDOW // CHUNK)
        first_valid = jnp.where(pl.program_id(1) == 0, WINDOW, 0)
        ok = (behind >= 0) & (behind <= WINDOW // CHUNK) & (col >= first_valid)
    kk, vv = [], []
    for kv in range(A_KV_HEADS):
        ks = slice(kv * A_HEAD_DIM, (kv + 1) * A_HEAD_DIM)
        vs = slice(A_KV_WIDTH + kv * A_HEAD_DIM, A_KV_WIDTH + (kv + 1) * A_HEAD_DIM)
        kk.append(jnp.concatenate([kp[:, ks], kvc[:, ks]], axis=0).astype(BF16))
        vv.append(jnp.concatenate([vp[:, ks], kvc[:, vs]], axis=0).astype(BF16))
    cols = [slice(h * A_HEAD_DIM, (h + 1) * A_HEAD_DIM) for h in range(A_HEADS)]
    scores = [_mm_nt(q_ref[:, cols[h]], kk[h // A_GROUP]) for h in range(A_HEADS)]
    es, dens = [], []
    for h in range(A_HEADS):
        s = scores[h] * A_SCALE
        if prompt:
            s = jnp.where(ok, s, NEG_BIG)
        sink = sink_ref[h]
        m = jnp.maximum(jnp.max(s, axis=-1, keepdims=True), sink)
        e = jnp.exp(s - m)
        es.append(e.astype(BF16))
        dens.append(jnp.sum(e, axis=-1, keepdims=True) + jnp.exp(sink - m))
    outs = [_mm(es[h], vv[h // A_GROUP]) for h in range(A_HEADS)]
    for h in range(A_HEADS):
        o_ref[:, cols[h]] = (outs[h] / dens[h]).astype(o_ref.dtype)
    if not prompt:
        nk_ref, nv_ref = cache_refs
        keep = WINDOW - qb
        nk_ref[0:keep, :] = kp[qb:, :]
        nk_ref[keep:, :] = kvc[:, 0:A_KV_WIDTH]
        nv_ref[0:keep, :] = vp[qb:, :]
        nv_ref[keep:, :] = kvc[:, A_KV_WIDTH:]


def _swa(sq, skv, k_cache, v_cache, sink, prompt):
    bsz, t, _ = sq.shape
    if prompt:
        qb = WINDOW
        assert t % qb == 0
        kp_spec = pl.BlockSpec((None, WINDOW, A_KV_WIDTH), lambda b, i: (b, jnp.maximum(i - 1, 0), 0))
        vp_spec = pl.BlockSpec((None, WINDOW, A_KV_WIDTH), lambda b, i: (b, jnp.maximum(i - 1, 0), 1))
        kp_arr, vp_arr = skv, skv
        out_shape = [jax.ShapeDtypeStruct((bsz, t, A_WIDTH), BF16)]
        out_specs = [pl.BlockSpec((None, qb, A_WIDTH), lambda b, i: (b, i, 0))]
    else:
        qb = t
        assert t <= WINDOW and t % (2 * SUBLANES) == 0 and k_cache.shape[1] == WINDOW
        kp_spec = pl.BlockSpec((None, WINDOW, A_KV_WIDTH), lambda b, i: (b, 0, 0))
        vp_spec = kp_spec
        kp_arr = k_cache.reshape(bsz, WINDOW, A_KV_WIDTH)
        vp_arr = v_cache.reshape(bsz, WINDOW, A_KV_WIDTH)
        out_shape = [jax.ShapeDtypeStruct((bsz, t, A_WIDTH), BF16),
                     jax.ShapeDtypeStruct((bsz, WINDOW, A_KV_WIDTH), F32),
                     jax.ShapeDtypeStruct((bsz, WINDOW, A_KV_WIDTH), F32)]
        cache_spec = pl.BlockSpec((None, WINDOW, A_KV_WIDTH), lambda b, i: (b, 0, 0))
        out_specs = [pl.BlockSpec((None, qb, A_WIDTH), lambda b, i: (b, i, 0)), cache_spec, cache_spec]
    outs = pl.pallas_call(
        functools.partial(_swa_kernel, prompt=prompt, qb=qb),
        grid=(bsz, t // qb),
        in_specs=[pl.BlockSpec((None, qb, A_WIDTH), lambda b, i: (b, i, 0)),
                  pl.BlockSpec((None, qb, 2 * A_KV_WIDTH), lambda b, i: (b, i, 0)),
                  kp_spec, vp_spec,
                  pl.BlockSpec(memory_space=pltpu.SMEM)],
        out_specs=out_specs,
        out_shape=out_shape,
        compiler_params=_cparams("parallel", "parallel"),
        name="swa",
    )(sq, skv, kp_arr, vp_arr, sink)
    if prompt:
        new_k = skv[:, t - WINDOW:, 0:A_KV_WIDTH]
        new_v = skv[:, t - WINDOW:, A_KV_WIDTH:]
    else:
        new_k, new_v = outs[1], outs[2]
    shape = (bsz, WINDOW, A_KV_HEADS, A_HEAD_DIM)
    return outs[0], new_k.reshape(shape), new_v.reshape(shape)


def _hgrn_kernel(x_ref, lb_ref, w_ref, s0_ref, o_ref, s_ref, st_ref, *, c):
    tb, tt, _ = x_ref.shape
    ti = pl.program_id(1)

    @pl.when(ti == 0)
    def _():
        for b in range(tb):
            for h in range(B_HEADS):
                st_ref[b, h] = s0_ref[b, h].T

    pos = lax.broadcasted_iota(jnp.int32, (tt, B_DIM), 0)
    r = lax.broadcasted_iota(jnp.int32, (c, c), 0)
    s = lax.broadcasted_iota(jnp.int32, (c, c), 1)
    x = r ^ s
    for b in range(tb):
        for h in range(B_HEADS):
            cs = slice(h * B_DIM, (h + 1) * B_DIM)
            qp = x_ref[b, :, cs]
            fp = x_ref[b, :, B_WIDTH + h * B_DIM:B_WIDTH + (h + 1) * B_DIM]
            v = x_ref[b, :, 2 * B_WIDTH + h * B_DIM:2 * B_WIDTH + (h + 1) * B_DIM]
            gp = x_ref[b, :, 3 * B_WIDTH + h * B_DIM:3 * B_WIDTH + (h + 1) * B_DIM]
            lb = lb_ref[:, cs]
            sig = _sigmoid(fp)
            f = jnp.maximum(lb + (1.0 - lb) * sig, F_MIN)
            k = (1.0 - lb) * (1.0 - sig)
            q = _silu(qp)
            total = f
            prefix = f
            suffix = jnp.ones_like(f)
            att = [jnp.where(r == s, jnp.sum((q * k)[j * c:(j + 1) * c], axis=-1, keepdims=True), 0.0)
                   for j in range(tt // c)]
            hh = 1
            while hh < c:
                qh = (q * prefix).astype(BF16)
                kh = (k * suffix).astype(BF16)
                level = (r > s) & (x >= hh) & (x < 2 * hh)
                for j in range(tt // c):
                    rows = slice(j * c, (j + 1) * c)
                    att[j] = jnp.where(level, _mm_nt(qh[rows], kh[rows]), att[j])
                upper = (pos % (2 * hh)) >= hh
                below = pltpu.roll(total, hh, axis=0)
                above = pltpu.roll(total, tt - hh, axis=0)
                prefix = prefix * jnp.where(upper, below, 1.0)
                suffix = suffix * jnp.where(upper, 1.0, above)
                total = total * jnp.where(upper, below, above)
                hh *= 2
            qc = (q * prefix).astype(BF16)
            kc = (k * suffix).astype(BF16)
            for j in range(tt // c):
                rows = slice(j * c, (j + 1) * c)
                st = st_ref[b, h]
                o = _mm_nt(qc[rows], st) + _mm(att[j], v[rows])
                st_ref[b, h] = st * total[(j + 1) * c - 1:(j + 1) * c, :] + _mm_tn(v[rows], kc[rows])
                o_ref[b, rows, cs] = _rms_gated(o, w_ref[...], gp[rows]).astype(o_ref.dtype)

    @pl.when(ti == pl.num_programs(1) - 1)
    def _():
        for b in range(tb):
            for h in range(B_HEADS):
                s_ref[b, h] = st_ref[b, h].T


def _hgrn(sb, lb, norm_w, state, tb, tt):
    bsz, t, _ = sb.shape
    c = min(CHUNK, t)
    assert tt % c == 0 and t % tt == 0 and bsz % tb == 0
    st_spec = pl.BlockSpec((tb, B_HEADS, B_DIM, B_DIM), lambda b, i: (b, 0, 0, 0))
    return pl.pallas_call(
        functools.partial(_hgrn_kernel, c=c),
        grid=(bsz // tb, t // tt),
        in_specs=[pl.BlockSpec((tb, tt, 4 * B_WIDTH), lambda b, i: (b, i, 0)),
                  pl.BlockSpec((1, B_WIDTH), lambda b, i: (0, 0)),
                  pl.BlockSpec((1, B_DIM), lambda b, i: (0, 0)),
                  st_spec],
        out_specs=[pl.BlockSpec((tb, tt, B_WIDTH), lambda b, i: (b, i, 0)), st_spec],
        out_shape=[jax.ShapeDtypeStruct((bsz, t, B_WIDTH), BF16),
                   jax.ShapeDtypeStruct((bsz, B_HEADS, B_DIM, B_DIM), F32)],
        scratch_shapes=[pltpu.VMEM((tb, B_HEADS, B_DIM, B_DIM), F32)],
        compiler_params=_cparams("parallel", "arbitrary"),
        name="hgrn",
    )(sb, lb.reshape(1, B_WIDTH), norm_w.reshape(1, B_DIM), state)


def _gdn_prep_kernel(x_ref, sm_ref, prev_ref, b0_ref, cw_ref, alog_ref, dtb_ref,
                     sv_ref, sk_ref, qk_ref, qt_ref, kt_ref, ec_ref, cbuf_ref, *, c):
    tb, tt, _ = x_ref.shape
    pad = SUBLANES
    keep = C_CONV - 1
    ti = pl.program_id(1)

    @pl.when(ti == 0)
    def _():
        cbuf_ref[:, pad - keep:pad, :] = b0_ref[...]

    @pl.when(ti > 0)
    def _():
        cbuf_ref[:, pad - keep:pad, :] = prev_ref[:, pad - keep:pad, :]

    cbuf_ref[:, pad:pad + tt, :] = x_ref[...]
    u = cbuf_ref[:, pad - keep:pad - keep + tt, :] * cw_ref[0:1, :]
    for j in range(1, C_CONV):
        u = u + cbuf_ref[:, pad - keep + j:pad - keep + j + tt, :] * cw_ref[j:j + 1, :]
    u = _silu(u)

    small = sm_ref[...]
    pre = small + dtb_ref[...]
    softplus = jnp.maximum(pre, 0.0) + jnp.log(1.0 + jnp.exp(-jnp.abs(pre)))
    g_all = -jnp.exp(alog_ref[...]) * softplus
    beta_all = _sigmoid(small)
    tr = lax.broadcasted_iota(jnp.int32, (tt, tt), 0)
    ts = lax.broadcasted_iota(jnp.int32, (tt, tt), 1)
    chunk_tri = jnp.where((tr >= ts) & (tr // c == ts // c), 1.0, 0.0).astype(F32)
    sel_r = lax.broadcasted_iota(jnp.int32, (SUBLANES, LANES), 0)
    sel_c = lax.broadcasted_iota(jnp.int32, (SUBLANES, LANES), 1)
    sel = jnp.where(sel_r == sel_c, 1.0, 0.0).astype(F32)
    r = lax.broadcasted_iota(jnp.int32, (c, c), 0)
    s = lax.broadcasted_iota(jnp.int32, (c, c), 1)
    incl = r >= s
    units = []
    for b in range(tb):
        cum_all = _mm_f32(chunk_tri, g_all[b])
        cum_rows = _mm_nt_f32(sel, cum_all)
        ecum_all = jnp.exp(cum_all)
        ec_ref[b] = ecum_all
        for h in range(C_HEADS):
            cs = slice(h * C_DIM, (h + 1) * C_DIM)
            uq = u[b, :, cs]
            uk = u[b, :, C_WIDTH + h * C_DIM:C_WIDTH + (h + 1) * C_DIM]
            v = u[b, :, 2 * C_WIDTH + h * C_DIM:2 * C_WIDTH + (h + 1) * C_DIM]
            q = uq * lax.rsqrt(jnp.sum(uq * uq, axis=-1, keepdims=True) + 1e-6) * (C_DIM ** -0.5)
            k = uk * lax.rsqrt(jnp.sum(uk * uk, axis=-1, keepdims=True) + 1e-6)
            beta = beta_all[b, :, h:h + 1]
            cum = cum_all[:, C_HEADS + h:C_HEADS + h + 1]
            ecum = ecum_all[:, C_HEADS + h:C_HEADS + h + 1]
            qt_ref[b, :, cs] = (q * ecum).astype(BF16)
            rhs = jnp.concatenate([v * beta, k * (beta * ecum)], axis=-1)
            for j in range(tt // c):
                rows = slice(j * c, (j + 1) * c)
                cum_j = cum[rows]
                cum_row = cum_rows[C_HEADS + h:C_HEADS + h + 1, rows]
                dec = jnp.where(incl, jnp.exp(jnp.where(incl, cum_j - cum_row, 0.0)), 0.0)
                k_j = k[rows]
                kb = k_j.astype(BF16)
                last = cum_j[c - 1:c, :]
                kt_ref[b, rows, cs] = (k_j * jnp.exp(last - cum_j)).astype(BF16)
                units.append(dict(b=b, rows=rows, cs=cs, qcols=slice(h * c, (h + 1) * c), dec=dec, kb=kb,
                                  qb=q[rows].astype(BF16), beta=beta[rows], rhs=rhs[rows]))
    for un in units:
        un["kk"] = _mm_nt(un["kb"], un["kb"])
        un["qk"] = _mm_nt(un["qb"], un["kb"])
    for un in units:
        qk_ref[un["b"], un["rows"], un["qcols"]] = (un["qk"] * un["dec"]).astype(BF16)
        n = jnp.where(r > s, un["beta"] * un["kk"] * un["dec"], 0.0)
        un["p"] = n
        un["corr"] = -n
    for _ in range(int(math.log2(c)) - 1):
        for un in units:
            un["p"] = _mm(un["p"], un["p"])
        for un in units:
            un["corr"] = un["corr"] + un["p"] + _mm(un["corr"], un["p"])
    for un in units:
        un["sol"] = un["rhs"] + _mm(un["corr"], un["rhs"])
    for un in units:
        sv_ref[un["b"], un["rows"], un["cs"]] = un["sol"][:, 0:C_DIM]
        sk_ref[un["b"], un["rows"], un["cs"]] = un["sol"][:, C_DIM:].astype(BF16)


def _gdn_scan_kernel(sv_ref, sk_ref, qk_ref, qt_ref, kt_ref, ec_ref, z_ref, w_ref, s0_ref,
                     o_ref, s_ref, st_ref, *, c):
    tb, tt, _ = sv_ref.shape
    ti = pl.program_id(1)

    @pl.when(ti == 0)
    def _():
        st_ref[...] = s0_ref[...]

    heads = [(b, h, slice(h * C_DIM, (h + 1) * C_DIM)) for b in range(tb) for h in range(C_HEADS)]
    for j in range(tt // c):
        rows = slice(j * c, (j + 1) * c)
        st = [st_ref[b, h] for b, h, _ in heads]
        stb = [a.astype(BF16) for a in st]
        sks = [_mm(sk_ref[b, rows, cs], stb[i]) for i, (b, h, cs) in enumerate(heads)]
        qs = [_mm(qt_ref[b, rows, cs], stb[i]) for i, (b, h, cs) in enumerate(heads)]
        wb = [(sv_ref[b, rows, cs] - sks[i]).astype(BF16) for i, (b, h, cs) in enumerate(heads)]
        upd = [_mm_tn(kt_ref[b, rows, cs], wb[i]) for i, (b, h, cs) in enumerate(heads)]
        qkw = [_mm(qk_ref[b, rows, h * c:(h + 1) * c], wb[i]) for i, (b, h, cs) in enumerate(heads)]
        for i, (b, h, cs) in enumerate(heads):
            e_last = ec_ref[b, (j + 1) * c - 1:(j + 1) * c, C_HEADS + h:C_HEADS + h + 1]
            st_ref[b, h] = e_last * st[i] + upd[i]
            o_ref[b, rows, cs] = _rms_gated(qs[i] + qkw[i], w_ref[...], z_ref[b, rows, cs]).astype(o_ref.dtype)

    @pl.when(ti == pl.num_programs(1) - 1)
    def _():
        s_ref[...] = st_ref[...]


def _gdn(sc, conv_w, a_log, dt_bias, norm_w, state, buf, tb_prep, tb_scan, tt):
    bsz, t, _ = sc.shape
    c = min(CHUNK, t)
    keep = C_CONV - 1
    assert t >= keep and tt % c == 0 and t % tt == 0
    pad_vec = lambda a: jnp.zeros((1, LANES), F32).at[0, C_HEADS:2 * C_HEADS].set(a.astype(F32))
    const = lambda shape: pl.BlockSpec(shape, lambda b, i: (0,) * len(shape))
    tile = lambda tb, w, col=0: pl.BlockSpec((tb, tt, w), lambda b, i: (b, i, col))
    shape = lambda w, dt: jax.ShapeDtypeStruct((bsz, t, w), dt)
    small_col = (C_QKV + C_WIDTH) // LANES
    prev_spec = pl.BlockSpec((tb_prep, SUBLANES, C_QKV),
                             lambda b, i: (b, jnp.maximum(i * (tt // SUBLANES) - 1, 0), 0))
    sv, sk, qk, qt, kt, ec = pl.pallas_call(
        functools.partial(_gdn_prep_kernel, c=c),
        grid=(bsz // tb_prep, t // tt),
        in_specs=[tile(tb_prep, C_QKV), tile(tb_prep, LANES, small_col), prev_spec,
                  pl.BlockSpec((tb_prep, keep, C_QKV), lambda b, i: (b, 0, 0)),
                  const((C_CONV, C_QKV)), const((1, LANES)), const((1, LANES))],
        out_specs=[tile(tb_prep, C_WIDTH), tile(tb_prep, C_WIDTH), tile(tb_prep, C_HEADS * c),
                   tile(tb_prep, C_WIDTH), tile(tb_prep, C_WIDTH), tile(tb_prep, LANES)],
        out_shape=[shape(C_WIDTH, F32), shape(C_WIDTH, BF16), shape(C_HEADS * c, BF16),
                   shape(C_WIDTH, BF16), shape(C_WIDTH, BF16), shape(LANES, F32)],
        scratch_shapes=[pltpu.VMEM((tb_prep, SUBLANES + tt, C_QKV), F32)],
        compiler_params=_cparams("parallel", "parallel"),
        name="gdn_prep",
    )(sc, sc, sc, buf, conv_w, pad_vec(a_log), pad_vec(dt_bias))
    st_spec = pl.BlockSpec((tb_scan, C_HEADS, C_DIM, C_DIM), lambda b, i: (b, 0, 0, 0))
    o, new_state = pl.pallas_call(
        functools.partial(_gdn_scan_kernel, c=c),
        grid=(bsz // tb_scan, t // tt),
        in_specs=[tile(tb_scan, C_WIDTH), tile(tb_scan, C_WIDTH), tile(tb_scan, C_HEADS * c),
                  tile(tb_scan, C_WIDTH), tile(tb_scan, C_WIDTH), tile(tb_scan, LANES),
                  tile(tb_scan, C_WIDTH, C_QKV // C_WIDTH), const((1, C_DIM)), st_spec],
        out_specs=[tile(tb_scan, C_WIDTH), st_spec],
        out_shape=[shape(C_WIDTH, BF16), jax.ShapeDtypeStruct((bsz, C_HEADS, C_DIM, C_DIM), F32)],
        scratch_shapes=[pltpu.VMEM((tb_scan, C_HEADS, C_DIM, C_DIM), F32)],
        compiler_params=_cparams("parallel", "arbitrary"),
        name="gdn_scan",
    )(sv, sk, qk, qt, kt, ec, sc, norm_w.reshape(1, C_DIM), state)
    return o, new_state, sc[:, t - keep:, 0:C_QKV]


def _merge_kernel(oa_ref, ob_ref, oc_ref, g_ref, x_ref, wa_ref, wb_ref, wc_ref, wo_ref,
                  lg_ref, lb_ref, y_ref, *, alpha):
    d = x_ref.shape[-1]
    mix = g_ref[:, 0:d].astype(F32) * _mm(oa_ref[...], wa_ref[...])
    mix = mix + g_ref[:, d:2 * d].astype(F32) * _mm(ob_ref[...], wb_ref[...])
    mix = mix + g_ref[:, 2 * d:3 * d].astype(F32) * _mm(oc_ref[...], wc_ref[...])
    y_ref[...] = _ln_rows(alpha * x_ref[...] + _mm(mix, wo_ref[...]), lg_ref[...], lb_ref[...])


def _merge(oa, ob, oc, gates, x, wba, wbb, wbc, wout, ln_g, ln_b, alpha):
    n, d = x.shape
    tm = min(256, n)
    row = lambda w: pl.BlockSpec((tm, w), lambda i: (i, 0))
    const = lambda a: pl.BlockSpec(a.shape, lambda i: (0, 0))
    return pl.pallas_call(
        functools.partial(_merge_kernel, alpha=alpha),
        grid=(n // tm,),
        in_specs=[row(A_WIDTH), row(B_WIDTH), row(C_WIDTH), row(N_BRANCH * d), row(d),
                  const(wba), const(wbb), const(wbc), const(wout),
                  pl.BlockSpec((1, d), lambda i: (0, 0)), pl.BlockSpec((1, d), lambda i: (0, 0))],
        out_specs=row(d),
        out_shape=jax.ShapeDtypeStruct((n, d), F32),
        compiler_params=_cparams("parallel"),
        name="merge",
    )(oa, ob, oc, gates, x, wba, wbb, wbc, wout, ln_g.reshape(1, d), ln_b.reshape(1, d))


def _ffn_kernel(x_ref, b0_ref, wu_ref, cw_ref, cb_ref, wd_ref, lg_ref, lb_ref,
                yf_ref, yb_ref, b_ref, carry_ref, win_ref, act_ref, *, alpha):
    tb, tt, d = x_ref.shape
    d_ff = wd_ref.shape[0]
    m = tb * tt
    nj = d_ff // FFN_COLS
    keep = F_CONV - 1
    pad = SUBLANES
    ti = pl.program_id(1)

    @pl.when(ti == 0)
    def _():
        carry_ref[...] = b0_ref[...]

    x = x_ref[...].reshape(m, d)
    xb = x.astype(BF16)

    def up(c0):
        return jnp.dot(xb, wu_ref[:, c0:c0 + FFN_COLS], preferred_element_type=F32)

    def conv(slot, c0, rows):
        cols = slice(c0, c0 + FFN_COLS)
        win_ref[slot, :, pad - keep:pad, :] = carry_ref[:, :, cols]
        win_ref[slot, :, pad:pad + tt, :] = rows.reshape(tb, tt, FFN_COLS)
        out = win_ref[slot, :, pad - keep:pad - keep + tt, :] * cw_ref[0:1, cols]
        for i in range(1, F_CONV):
            out = out + win_ref[slot, :, pad - keep + i:pad - keep + i + tt, :] * cw_ref[i:i + 1, cols]
        carry_ref[:, :, cols] = win_ref[slot, :, pad + tt - keep:pad + tt, :]
        return out + cb_ref[:, cols]

    nxt = (up(0), up(d_ff))
    for j in range(nj):
        cur = nxt
        if j + 1 < nj:
            nxt = (up((j + 1) * FFN_COLS), up(d_ff + (j + 1) * FFN_COLS))
        gate = conv(2 * (j % 2), j * FFN_COLS, cur[0])
        val = conv(2 * (j % 2) + 1, d_ff + j * FFN_COLS, cur[1])
        act_ref[:, j * FFN_COLS:(j + 1) * FFN_COLS] = (_silu(gate) * val).reshape(m, FFN_COLS).astype(BF16)
    down = jnp.dot(act_ref[...], wd_ref[...], preferred_element_type=F32)
    y = _ln_rows(alpha * x + down, lg_ref[...], lb_ref[...]).reshape(tb, tt, d)
    yf_ref[...] = y
    yb_ref[...] = y.astype(BF16)

    @pl.when(ti == pl.num_programs(1) - 1)
    def _():
        b_ref[...] = carry_ref[...]


def _ffn(x, buf, wu, cw, cb, wd, ln_g, ln_b, alpha, tb, tt):
    bsz, t, d = x.shape
    d_ff = wd.shape[0]
    keep = F_CONV - 1
    row = pl.BlockSpec((tb, tt, d), lambda b, i: (b, i, 0))
    buf_spec = pl.BlockSpec((tb, keep, 2 * d_ff), lambda b, i: (b, 0, 0))
    const = lambda a: pl.BlockSpec(a.shape, lambda b, i: (0,) * a.ndim, pipeline_mode=pl.Buffered(1))
    lg = ln_g.reshape(1, d)
    lb = ln_b.reshape(1, d)
    cb = cb.reshape(1, 2 * d_ff)
    return pl.pallas_call(
        functools.partial(_ffn_kernel, alpha=alpha),
        grid=(bsz // tb, t // tt),
        in_specs=[row, buf_spec, const(wu), const(cw), const(cb), const(wd), const(lg), const(lb)],
        out_specs=[row, row, buf_spec],
        out_shape=[jax.ShapeDtypeStruct((bsz, t, d), F32),
                   jax.ShapeDtypeStruct((bsz, t, d), BF16),
                   jax.ShapeDtypeStruct((bsz, keep, 2 * d_ff), F32)],
        scratch_shapes=[pltpu.VMEM((tb, keep, 2 * d_ff), F32),
                        pltpu.VMEM((4, tb, SUBLANES + tt, FFN_COLS), F32),
                        pltpu.VMEM((tb * tt, d_ff), BF16)],
        compiler_params=_cparams("parallel", "arbitrary"),
        name="conv_ffn",
    )(x, buf, wu, cw, cb, wd, lg, lb)


def _layer(xf, xb, state, p, prompt, alpha):
    k_cache, v_cache, s_hgrn, s_gdn, buf_gdn, buf_ffn = state
    bsz, t, d = xf.shape
    n = bsz * t
    x2 = xb.reshape(n, d)
    sq, skv, sb, sc, sg = _in_proj(x2, p["wq"], p["wkv"], p["wb"], p["wc"], p["wg"])
    sq, skv, sb, sc = (a.reshape(bsz, t, -1) for a in (sq, skv, sb, sc))
    oa, new_k, new_v = _swa(sq, skv, k_cache, v_cache, p["sink"], prompt)
    ob, new_hgrn = _hgrn(sb, p["lb"], p["hgrn_w"], s_hgrn, *((1, 256) if prompt else (8, t)))
    gdn_tiles = (1, bsz, 256) if prompt else (8, 8, t)
    oc, new_gdn, new_gconv = _gdn(sc, p["gdn_conv_w"], p["a_log"], p["dt_bias"], p["gdn_w"], s_gdn, buf_gdn,
                                  *gdn_tiles)
    y = _merge(oa.reshape(n, -1), ob.reshape(n, -1), oc.reshape(n, -1), sg, xf.reshape(n, d),
               p["wba"], p["wbb"], p["wbc"], p["wout"], p["ln1_g"], p["ln1_b"], alpha)
    tb, tt = (1, 256) if prompt else (8, t)
    zf, zb, new_fconv = _ffn(y.reshape(bsz, t, d), buf_ffn, p["wu"], p["fcw"], p["fcb"], p["wd"],
                             p["ln2_g"], p["ln2_b"], alpha, tb, tt)
    return zf, zb, (new_k, new_v, new_hgrn, new_gdn, new_gconv, new_fconv)


def kernel(x_prompt, x_sample, cache_swa_k, cache_swa_v, state_hgrn, state_gdn, state_gdn_conv, state_ffn_conv, ln_in_g, ln_in_b, w_in, attn_sinks, hgrn_lb_logits, hgrn_norm_w, gdn_conv_w, gdn_a_log, gdn_dt_bias, gdn_norm_w, w_branch, w_out, ln1_g, ln1_b, w_up, ffn_conv_w, ffn_conv_b, w_down, ln2_g, ln2_b):
    depth, d, _ = w_in.shape
    alpha = (2 * depth) ** 0.25
    lb_sm = jax.nn.softmax(hgrn_lb_logits.astype(F32), axis=0)
    lower_bounds = jnp.cumsum(lb_sm, axis=0) - lb_sm[0]

    o_a = A_WIDTH + 2 * A_KV_WIDTH
    o_b = o_a + 4 * B_WIDTH
    o_c = o_b + C_QKV + C_WIDTH
    o_s = o_c + 2 * C_HEADS
    wq = w_in[:, :, 0:A_WIDTH].astype(BF16)
    wkv = w_in[:, :, A_WIDTH:o_a].astype(BF16)
    wb = w_in[:, :, o_a:o_b].astype(BF16)
    wc = jnp.concatenate([w_in[:, :, o_b:o_s], jnp.zeros((depth, d, LANES - 2 * C_HEADS), w_in.dtype)],
                         axis=-1).astype(BF16)
    wg = w_in[:, :, o_s:].astype(BF16)
    wba = w_branch[:, 0:A_WIDTH].astype(BF16)
    wbb = w_branch[:, A_WIDTH:A_WIDTH + B_WIDTH].astype(BF16)
    wbc = w_branch[:, A_WIDTH + B_WIDTH:].astype(BF16)
    wout = w_out.astype(BF16)
    wu, fcw, fcb, wd = w_up.astype(BF16), ffn_conv_w.astype(F32), ffn_conv_b.astype(F32), w_down.astype(BF16)

    hp_f, hp_b = _layer_norm(x_prompt, ln_in_g, ln_in_b)
    hs_f, hs_b = _layer_norm(x_sample, ln_in_g, ln_in_b)
    bp = x_prompt.shape[0]
    prompt_state = (None, None,
                    jnp.zeros((bp, B_HEADS, B_DIM, B_DIM), F32),
                    jnp.zeros((bp, C_HEADS, C_DIM, C_DIM), F32),
                    jnp.zeros((bp, C_CONV - 1, C_QKV), F32),
                    jnp.zeros((bp, F_CONV - 1, w_up.shape[-1]), F32))
    p_new, s_new = [], []
    for l in range(depth):
        p = dict(wq=wq[l], wkv=wkv[l], wb=wb[l], wc=wc[l], wg=wg[l], sink=attn_sinks[l].astype(F32), lb=lower_bounds[l],
                 hgrn_w=hgrn_norm_w[l], gdn_conv_w=gdn_conv_w[l], a_log=gdn_a_log[l], dt_bias=gdn_dt_bias[l],
                 gdn_w=gdn_norm_w[l], wba=wba[l], wbb=wbb[l], wbc=wbc[l], wout=wout[l],
                 ln1_g=ln1_g[l], ln1_b=ln1_b[l], wu=wu[l], fcw=fcw[l], fcb=fcb[l], wd=wd[l],
                 ln2_g=ln2_g[l], ln2_b=ln2_b[l])
        hp_f, hp_b, st_p = _layer(hp_f, hp_b, prompt_state, p, True, alpha)
        sample_state = (cache_swa_k[l], cache_swa_v[l], state_hgrn[l], state_gdn[l],
                        state_gdn_conv[l], state_ffn_conv[l])
        hs_f, hs_b, st_s = _layer(hs_f, hs_b, sample_state, p, False, alpha)
        p_new.append(st_p)
        s_new.append(st_s)

    def stack(states, i):
        return jnp.stack([st[i] for st in states])

    return (hp_f, hs_f) + tuple(stack(p_new, i) for i in range(6)) + tuple(stack(s_new, i) for i in range(6))
```

```python
import functools
import math

import jax
import jax.numpy as jnp
from jax import lax
from jax.experimental import pallas as pl
from jax.experimental.pallas import tpu as pltpu

F32 = jnp.float32
BF16 = jnp.bfloat16

CHUNK = 64
WINDOW = 128
A_HEADS = 8
A_KV_HEADS = 2
A_GROUP = A_HEADS // A_KV_HEADS
A_HEAD_DIM = 64
A_SCALE = A_HEAD_DIM ** -0.5
A_WIDTH = A_HEADS * A_HEAD_DIM
A_KV_WIDTH = A_KV_HEADS * A_HEAD_DIM
B_HEADS = 4
B_DIM = 128
C_HEADS = 4
C_DIM = 128
C_CONV = 4
F_CONV = 3
N_BRANCH = 3
B_WIDTH = B_HEADS * B_DIM
C_WIDTH = C_HEADS * C_DIM
C_QKV = 3 * C_WIDTH
LN_EPS = 1e-5
RMS_EPS = 1e-6
NEG_BIG = -1e30
F_MIN = 1e-30

LANES = 128
SUBLANES = 8
FFN_COLS = 256
VMEM_LIMIT = 48 * 1024 * 1024


def _cparams(*sem):
    return pltpu.CompilerParams(dimension_semantics=sem, vmem_limit_bytes=VMEM_LIMIT)


def _resident(a, layer, row_block=None, rows=None):
    shape = (None, a.shape[1] if rows is None else rows, a.shape[2])
    rb = 0 if row_block is None else row_block
    return pl.BlockSpec(shape, lambda *_: (layer, rb, 0), pipeline_mode=pl.Buffered(1))


def _mm(a, b):
    return jnp.dot(a.astype(BF16), b.astype(BF16), preferred_element_type=F32)


def _mm_nt(a, b):
    return lax.dot_general(a.astype(BF16), b.astype(BF16), (((1,), (1,)), ((), ())),
                           preferred_element_type=F32)


def _mm_tn(a, b):
    return lax.dot_general(a.astype(BF16), b.astype(BF16), (((0,), (0,)), ((), ())),
                           preferred_element_type=F32)


def _mm_f32(a, b):
    return jnp.dot(a, b, preferred_element_type=F32, precision=lax.Precision.HIGHEST)


def _mm_nt_f32(a, b):
    return lax.dot_general(a, b, (((1,), (1,)), ((), ())), preferred_element_type=F32,
                           precision=lax.Precision.HIGHEST)


def _sigmoid(x):
    return 1.0 / (1.0 + jnp.exp(-x))


def _silu(x):
    return x * _sigmoid(x)


def _ln_rows(x, g, b):
    mu = jnp.mean(x, axis=-1, keepdims=True)
    xc = x - mu
    var = jnp.mean(xc * xc, axis=-1, keepdims=True)
    return xc * lax.rsqrt(var + LN_EPS) * g + b


def _rms_gated(o, w, z):
    n = o * lax.rsqrt(jnp.mean(o * o, axis=-1, keepdims=True) + RMS_EPS)
    return n * w * _silu(z)


def _shift_rows(win, sft):
    rows, cols = win.shape
    groups = rows // SUBLANES
    rot = pltpu.roll(win.reshape(groups, SUBLANES, cols), sft, axis=1)
    above = jnp.concatenate([rot[groups - 1:], rot[:groups - 1]], axis=0)
    sub = lax.broadcasted_iota(jnp.int32, rot.shape, 1)
    return jnp.where(sub < sft, above, rot).reshape(rows, cols)


def _tri(c, lower):
    r = lax.broadcasted_iota(jnp.int32, (c, c), 0)
    s = lax.broadcasted_iota(jnp.int32, (c, c), 1)
    return jnp.where(r >= s if lower else r <= s, 1.0, 0.0).astype(F32)


def _ln_kernel(x_ref, g_ref, b_ref, of_ref, ob_ref):
    y = _ln_rows(x_ref[...], g_ref[...], b_ref[...])
    of_ref[...] = y
    ob_ref[...] = y.astype(BF16)


def _layer_norm(x, g, b):
    bsz, t, d = x.shape
    n = bsz * t
    tm = min(512, n)
    of, ob = pl.pallas_call(
        _ln_kernel,
        grid=(n // tm,),
        in_specs=[pl.BlockSpec((tm, d), lambda i: (i, 0)),
                  pl.BlockSpec((1, d), lambda i: (0, 0)),
                  pl.BlockSpec((1, d), lambda i: (0, 0))],
        out_specs=[pl.BlockSpec((tm, d), lambda i: (i, 0)),
                   pl.BlockSpec((tm, d), lambda i: (i, 0))],
        out_shape=[jax.ShapeDtypeStruct((n, d), F32), jax.ShapeDtypeStruct((n, d), BF16)],
        compiler_params=_cparams("parallel"),
        name="input_ln",
    )(x.reshape(n, d), g.reshape(1, d), b.reshape(1, d))
    return of.reshape(bsz, t, d), ob.reshape(bsz, t, d)


def _in_proj_kernel(x_ref, wq_ref, wkv_ref, wb_ref, wc_ref, wg_ref, q_ref, kv_ref, b_ref, c_ref, g_ref):
    x = x_ref[...]
    dot = lambda w_ref: jnp.dot(x, w_ref[...], preferred_element_type=F32)
    q_ref[...] = dot(wq_ref).astype(BF16)
    kv_ref[...] = dot(wkv_ref)
    b_ref[...] = dot(wb_ref)
    c_ref[...] = dot(wc_ref)
    g_ref[...] = _sigmoid(dot(wg_ref)).astype(BF16)


def _in_proj(x, ws, layer):
    n, k = x.shape
    tm = min(256, n)
    dts = (BF16, F32, F32, F32, BF16)
    return pl.pallas_call(
        _in_proj_kernel,
        grid=(n // tm,),
        in_specs=[pl.BlockSpec((tm, k), lambda i: (i, 0))] + [_resident(w, layer) for w in ws],
        out_specs=[pl.BlockSpec((tm, w.shape[2]), lambda i: (i, 0)) for w in ws],
        out_shape=[jax.ShapeDtypeStruct((n, w.shape[2]), dt) for w, dt in zip(ws, dts)],
        compiler_params=_cparams("parallel"),
        name="in_proj",
    )(x, *ws)


def _swa_kernel(q_ref, kvc_ref, kp_ref, vp_ref, sink_ref, o_ref, *cache_refs, prompt, qb, nsub):
    kvc = kvc_ref[...]
    kp = kp_ref[...]
    vp = vp_ref[...]
    nkeys = WINDOW + qb
    if prompt:
        row = lax.broadcasted_iota(jnp.int32, (qb, nkeys), 0)
        col = lax.broadcasted_iota(jnp.int32, (qb, nkeys), 1)
        behind = row // CHUNK - (col // CHUNK - WINDOW // CHUNK)
        band = (behind >= 0) & (behind <= WINDOW // CHUNK)
        first_valid = jnp.where(pl.program_id(1) == 0, WINDOW, 0)
        band_first = band & (col >= first_valid)
    cols = [slice(h * A_HEAD_DIM, (h + 1) * A_HEAD_DIM) for h in range(A_HEADS)]
    units = []
    for sub in range(nsub):
        rows = slice(sub * qb, (sub + 1) * qb)
        before = slice((sub - 1) * qb, sub * qb)
        for kv in range(A_KV_HEADS):
            ks = slice(kv * A_HEAD_DIM, (kv + 1) * A_HEAD_DIM)
            vs = slice(A_KV_WIDTH + kv * A_HEAD_DIM, A_KV_WIDTH + (kv + 1) * A_HEAD_DIM)
            k_prev, v_prev = (kp[:, ks], vp[:, ks]) if sub == 0 else (kvc[before, ks], kvc[before, vs])
            kk = jnp.concatenate([k_prev, kvc[rows, ks]], axis=0).astype(BF16)
            vv = jnp.concatenate([v_prev, kvc[rows, vs]], axis=0).astype(BF16)
            units += [(sub, kv * A_GROUP + g, rows, kk, vv) for g in range(A_GROUP)]
    scores = [_mm_nt(q_ref[rows, cols[h]], kk) for _, h, rows, kk, _ in units]
    es, dens = [], []
    for (sub, h, _, _, _), s in zip(units, scores):
        s = s * A_SCALE
        if prompt:
            s = jnp.where(band_first if sub == 0 else band, s, NEG_BIG)
        sink = sink_ref[h]
        m = jnp.maximum(jnp.max(s, axis=-1, keepdims=True), sink)
        e = jnp.exp(s - m)
        es.append(e.astype(BF16))
        dens.append(jnp.sum(e, axis=-1, keepdims=True) + jnp.exp(sink - m))
    outs = [_mm(e, vv) for e, (_, _, _, _, vv) in zip(es, units)]
    for (_, h, rows, _, _), o, den in zip(units, outs, dens):
        o_ref[rows, cols[h]] = (o / den).astype(o_ref.dtype)
    if not prompt:
        nk_ref, nv_ref = cache_refs
        keep = WINDOW - qb
        nk_ref[0:keep, :] = kp[qb:, :]
        nk_ref[keep:, :] = kvc[:, 0:A_KV_WIDTH]
        nv_ref[0:keep, :] = vp[qb:, :]
        nv_ref[keep:, :] = kvc[:, A_KV_WIDTH:]


def _swa(sq, skv, k_cache, v_cache, sink, prompt):
    bsz, t, _ = sq.shape
    if prompt:
        qb = WINDOW
        nsub = 2 if t % (2 * qb) == 0 else 1
        assert t % (nsub * qb) == 0
        kp_spec = pl.BlockSpec((None, WINDOW, A_KV_WIDTH), lambda b, i: (b, jnp.maximum(i * nsub - 1, 0), 0))
        vp_spec = pl.BlockSpec((None, WINDOW, A_KV_WIDTH), lambda b, i: (b, jnp.maximum(i * nsub - 1, 0), 1))
        kp_arr, vp_arr = skv, skv
        out_shape = [jax.ShapeDtypeStruct((bsz, t, A_WIDTH), BF16)]
        out_specs = [pl.BlockSpec((None, nsub * qb, A_WIDTH), lambda b, i: (b, i, 0))]
    else:
        qb = t
        nsub = 1
        assert t <= WINDOW and t % (2 * SUBLANES) == 0 and k_cache.shape[1] == WINDOW
        kp_spec = pl.BlockSpec((None, WINDOW, A_KV_WIDTH), lambda b, i: (b, 0, 0))
        vp_spec = kp_spec
        kp_arr = k_cache.reshape(bsz, WINDOW, A_KV_WIDTH)
        vp_arr = v_cache.reshape(bsz, WINDOW, A_KV_WIDTH)
        out_shape = [jax.ShapeDtypeStruct((bsz, t, A_WIDTH), BF16),
                     jax.ShapeDtypeStruct((bsz, WINDOW, A_KV_WIDTH), F32),
                     jax.ShapeDtypeStruct((bsz, WINDOW, A_KV_WIDTH), F32)]
        cache_spec = pl.BlockSpec((None, WINDOW, A_KV_WIDTH), lambda b, i: (b, 0, 0))
        out_specs = [pl.BlockSpec((None, qb, A_WIDTH), lambda b, i: (b, i, 0)), cache_spec, cache_spec]
    outs = pl.pallas_call(
        functools.partial(_swa_kernel, prompt=prompt, qb=qb, nsub=nsub),
        grid=(bsz, t // (nsub * qb)),
        in_specs=[pl.BlockSpec((None, nsub * qb, A_WIDTH), lambda b, i: (b, i, 0)),
                  pl.BlockSpec((None, nsub * qb, 2 * A_KV_WIDTH), lambda b, i: (b, i, 0)),
                  kp_spec, vp_spec,
                  pl.BlockSpec(memory_space=pltpu.SMEM)],
        out_specs=out_specs,
        out_shape=out_shape,
        compiler_params=_cparams("parallel", "parallel"),
        name="swa",
    )(sq, skv, kp_arr, vp_arr, sink)
    if prompt:
        new_k = skv[:, t - WINDOW:, 0:A_KV_WIDTH]
        new_v = skv[:, t - WINDOW:, A_KV_WIDTH:]
    else:
        new_k, new_v = outs[1], outs[2]
    shape = (bsz, WINDOW, A_KV_HEADS, A_HEAD_DIM)
    return outs[0], new_k.reshape(shape), new_v.reshape(shape)


def _hgrn_kernel(x_ref, lb_ref, w_ref, s0_ref, o_ref, s_ref, st_ref, *, c):
    tb, tt, _ = x_ref.shape
    ti = pl.program_id(1)

    @pl.when(ti == 0)
    def _():
        for b in range(tb):
            for h in range(B_HEADS):
                st_ref[b, h] = s0_ref[b, h].T

    pos = lax.broadcasted_iota(jnp.int32, (tt, B_DIM), 0)
    r = lax.broadcasted_iota(jnp.int32, (c, c), 0)
    s = lax.broadcasted_iota(jnp.int32, (c, c), 1)
    x = r ^ s
    sizes = [1 << i for i in range(int(math.log2(c)))]
    levels = {hh: (r > s) & (x >= hh) & (x < 2 * hh) for hh in sizes}
    uppers = {hh: (pos % (2 * hh)) >= hh for hh in sizes}
    for b in range(tb):
        for h in range(B_HEADS):
            cs = slice(h * B_DIM, (h + 1) * B_DIM)
            qp = x_ref[b, :, cs]
            fp = x_ref[b, :, B_WIDTH + h * B_DIM:B_WIDTH + (h + 1) * B_DIM]
            v = x_ref[b, :, 2 * B_WIDTH + h * B_DIM:2 * B_WIDTH + (h + 1) * B_DIM]
            gp = x_ref[b, :, 3 * B_WIDTH + h * B_DIM:3 * B_WIDTH + (h + 1) * B_DIM]
            lb = lb_ref[:, cs]
            sig = _sigmoid(fp)
            f = jnp.maximum(lb + (1.0 - lb) * sig, F_MIN)
            k = (1.0 - lb) * (1.0 - sig)
            q = _silu(qp)
            total = f
            qd = q * f
            kd = k
            att = [jnp.where(r == s, jnp.sum((q * k)[j * c:(j + 1) * c], axis=-1, keepdims=True), 0.0)
                   for j in range(tt // c)]
            hh = 1
            while hh < c:
                qh = qd.astype(BF16)
                kh = kd.astype(BF16)
                for j in range(tt // c):
                    rows = slice(j * c, (j + 1) * c)
                    att[j] = jnp.where(levels[hh], _mm_nt(qh[rows], kh[rows]), att[j])
                below = pltpu.roll(total, hh, axis=0)
                above = pltpu.roll(total, tt - hh, axis=0)
                qd = jnp.where(uppers[hh], qd * below, qd)
                kd = jnp.where(uppers[hh], kd, kd * above)
                total = total * jnp.where(uppers[hh], below, above)
                hh *= 2
            qc = qd.astype(BF16)
            kc = kd.astype(BF16)
            for j in range(tt // c):
                rows = slice(j * c, (j + 1) * c)
                st = st_ref[b, h]
                o = _mm_nt(qc[rows], st) + _mm(att[j], v[rows])
                st_ref[b, h] = st * total[(j + 1) * c - 1:(j + 1) * c, :] + _mm_tn(v[rows], kc[rows])
                o_ref[b, rows, cs] = _rms_gated(o, w_ref[...], gp[rows]).astype(o_ref.dtype)

    @pl.when(ti == pl.num_programs(1) - 1)
    def _():
        for b in range(tb):
            for h in range(B_HEADS):
                s_ref[b, h] = st_ref[b, h].T


def _hgrn(sb, lb, norm_w, state, tb, tt):
    bsz, t, _ = sb.shape
    c = min(CHUNK, t)
    assert tt % c == 0 and t % tt == 0 and bsz % tb == 0
    st_spec = pl.BlockSpec((tb, B_HEADS, B_DIM, B_DIM), lambda b, i: (b, 0, 0, 0))
    return pl.pallas_call(
        functools.partial(_hgrn_kernel, c=c),
        grid=(bsz // tb, t // tt),
        in_specs=[pl.BlockSpec((tb, tt, 4 * B_WIDTH), lambda b, i: (b, i, 0)),
                  pl.BlockSpec((1, B_WIDTH), lambda b, i: (0, 0)),
                  pl.BlockSpec((1, B_DIM), lambda b, i: (0, 0)),
                  st_spec],
        out_specs=[pl.BlockSpec((tb, tt, B_WIDTH), lambda b, i: (b, i, 0)), st_spec],
        out_shape=[jax.ShapeDtypeStruct((bsz, t, B_WIDTH), BF16),
                   jax.ShapeDtypeStruct((bsz, B_HEADS, B_DIM, B_DIM), F32)],
        scratch_shapes=[pltpu.VMEM((tb, B_HEADS, B_DIM, B_DIM), F32)],
        compiler_params=_cparams("parallel", "arbitrary"),
        name="hgrn",
    )(sb, lb.reshape(1, B_WIDTH), norm_w.reshape(1, B_DIM), state)


def _gdn_prep_kernel(x_ref, sm_ref, prev_ref, b0_ref, cw_ref, alog_ref, dtb_ref,
                     sv_ref, sk_ref, qk_ref, qt_ref, kt_ref, ec_ref, cbuf_ref, *, c):
    tb, tt, _ = x_ref.shape
    pad = SUBLANES
    keep = C_CONV - 1
    ti = pl.program_id(1)

    @pl.when(ti == 0)
    def _():
        cbuf_ref[:, pad - keep:pad, :] = b0_ref[...]

    @pl.when(ti > 0)
    def _():
        cbuf_ref[:, pad - keep:pad, :] = prev_ref[:, pad - keep:pad, :]

    cbuf_ref[:, 0:pad - keep, :] = jnp.zeros((tb, pad - keep, C_QKV), F32)
    cbuf_ref[:, pad:pad + tt, :] = x_ref[...]
    win = cbuf_ref[...].reshape(tb * (pad + tt), C_QKV)
    u = win * cw_ref[keep:keep + 1, :]
    for sft in range(1, C_CONV):
        u = u + _shift_rows(win, sft) * cw_ref[keep - sft:keep - sft + 1, :]
    u = _silu(u.reshape(tb, pad + tt, C_QKV)[:, pad:, :])

    small = sm_ref[...]
    pre = small + dtb_ref[...]
    softplus = jnp.maximum(pre, 0.0) + jnp.log(1.0 + jnp.exp(-jnp.abs(pre)))
    g_all = -jnp.exp(alog_ref[...]) * softplus
    beta_all = _sigmoid(small)
    tr = lax.broadcasted_iota(jnp.int32, (tt, tt), 0)
    ts = lax.broadcasted_iota(jnp.int32, (tt, tt), 1)
    chunk_tri = jnp.where((tr >= ts) & (tr // c == ts // c), 1.0, 0.0).astype(F32)
    sel_r = lax.broadcasted_iota(jnp.int32, (SUBLANES, LANES), 0)
    sel_c = lax.broadcasted_iota(jnp.int32, (SUBLANES, LANES), 1)
    sel = jnp.where(sel_r == sel_c, 1.0, 0.0).astype(F32)
    r = lax.broadcasted_iota(jnp.int32, (c, c), 0)
    s = lax.broadcasted_iota(jnp.int32, (c, c), 1)
    incl = r >= s
    units = []
    for b in range(tb):
        cum_all = _mm_f32(chunk_tri, g_all[b])
        cum_rows = _mm_nt_f32(sel, cum_all)
        ecum_all = jnp.exp(cum_all)
        ec_ref[b] = ecum_all
        for h in range(C_HEADS):
            cs = slice(h * C_DIM, (h + 1) * C_DIM)
            uq = u[b, :, cs]
            uk = u[b, :, C_WIDTH + h * C_DIM:C_WIDTH + (h + 1) * C_DIM]
            v = u[b, :, 2 * C_WIDTH + h * C_DIM:2 * C_WIDTH + (h + 1) * C_DIM]
            q = uq * lax.rsqrt(jnp.sum(uq * uq, axis=-1, keepdims=True) + 1e-6) * (C_DIM ** -0.5)
            k = uk * lax.rsqrt(jnp.sum(uk * uk, axis=-1, keepdims=True) + 1e-6)
            beta = beta_all[b, :, h:h + 1]
            cum = cum_all[:, C_HEADS + h:C_HEADS + h + 1]
            ecum = ecum_all[:, C_HEADS + h:C_HEADS + h + 1]
            qt_ref[b, :, cs] = (q * ecum).astype(BF16)
            rhs = jnp.concatenate([v * beta, k * (beta * ecum)], axis=-1)
            for j in range(tt // c):
                rows = slice(j * c, (j + 1) * c)
                cum_j = cum[rows]
                cum_row = cum_rows[C_HEADS + h:C_HEADS + h + 1, rows]
                dec = jnp.where(incl, jnp.exp(jnp.where(incl, cum_j - cum_row, 0.0)), 0.0)
                k_j = k[rows]
                kb = k_j.astype(BF16)
                last = cum_j[c - 1:c, :]
                kt_ref[b, rows, cs] = (k_j * jnp.exp(last - cum_j)).astype(BF16)
                units.append(dict(b=b, rows=rows, cs=cs, qcols=slice(h * c, (h + 1) * c), dec=dec, kb=kb,
                                  qb=q[rows].astype(BF16), beta=beta[rows], rhs=rhs[rows]))
    for un in units:
        un["kk"] = _mm_nt(un["kb"], un["kb"])
        un["qk"] = _mm_nt(un["qb"], un["kb"])
    for un in units:
        qk_ref[un["b"], un["rows"], un["qcols"]] = (un["qk"] * un["dec"]).astype(BF16)
        n = jnp.where(r > s, un["beta"] * un["kk"] * un["dec"], 0.0)
        un["p"] = n
        un["corr"] = -n
    for _ in range(int(math.log2(c)) - 1):
        for un in units:
            un["p"] = _mm(un["p"], un["p"])
        for un in units:
            un["corr"] = un["corr"] + un["p"] + _mm(un["corr"], un["p"])
    for un in units:
        un["sol"] = un["rhs"] + _mm(un["corr"], un["rhs"])
    for un in units:
        sv_ref[un["b"], un["rows"], un["cs"]] = un["sol"][:, 0:C_DIM]
        sk_ref[un["b"], un["rows"], un["cs"]] = un["sol"][:, C_DIM:].astype(BF16)


def _gdn_scan_kernel(sv_ref, sk_ref, qk_ref, qt_ref, kt_ref, ec_ref, z_ref, w_ref, s0_ref,
                     o_ref, s_ref, st_ref, *, c):
    tb, tt, _ = sv_ref.shape
    ti = pl.program_id(1)

    @pl.when(ti == 0)
    def _():
        st_ref[...] = s0_ref[...]

    heads = [(b, h, slice(h * C_DIM, (h + 1) * C_DIM)) for b in range(tb) for h in range(C_HEADS)]
    for j in range(tt // c):
        rows = slice(j * c, (j + 1) * c)
        st = [st_ref[b, h] for b, h, _ in heads]
        stb = [a.astype(BF16) for a in st]
        sks = [_mm(sk_ref[b, rows, cs], stb[i]) for i, (b, h, cs) in enumerate(heads)]
        qs = [_mm(qt_ref[b, rows, cs], stb[i]) for i, (b, h, cs) in enumerate(heads)]
        wb = [(sv_ref[b, rows, cs] - sks[i]).astype(BF16) for i, (b, h, cs) in enumerate(heads)]
        upd = [_mm_tn(kt_ref[b, rows, cs], wb[i]) for i, (b, h, cs) in enumerate(heads)]
        qkw = [_mm(qk_ref[b, rows, h * c:(h + 1) * c], wb[i]) for i, (b, h, cs) in enumerate(heads)]
        for i, (b, h, cs) in enumerate(heads):
            e_last = ec_ref[b, (j + 1) * c - 1:(j + 1) * c, C_HEADS + h:C_HEADS + h + 1]
            st_ref[b, h] = e_last * st[i] + upd[i]
            o_ref[b, rows, cs] = _rms_gated(qs[i] + qkw[i], w_ref[...], z_ref[b, rows, cs]).astype(o_ref.dtype)

    @pl.when(ti == pl.num_programs(1) - 1)
    def _():
        s_ref[...] = st_ref[...]


def _gdn(sc, conv_w, a_log, dt_bias, norm_w, state, buf, tb_prep, tb_scan, tt):
    bsz, t, _ = sc.shape
    c = min(CHUNK, t)
    keep = C_CONV - 1
    assert t >= keep and tt % c == 0 and t % tt == 0
    pad_vec = lambda a: jnp.zeros((1, LANES), F32).at[0, C_HEADS:2 * C_HEADS].set(a.astype(F32))
    const = lambda shape: pl.BlockSpec(shape, lambda b, i: (0,) * len(shape))
    tile = lambda tb, w, col=0: pl.BlockSpec((tb, tt, w), lambda b, i: (b, i, col))
    shape = lambda w, dt: jax.ShapeDtypeStruct((bsz, t, w), dt)
    small_col = (C_QKV + C_WIDTH) // LANES
    prev_spec = pl.BlockSpec((tb_prep, SUBLANES, C_QKV),
                             lambda b, i: (b, jnp.maximum(i * (tt // SUBLANES) - 1, 0), 0))
    sv, sk, qk, qt, kt, ec = pl.pallas_call(
        functools.partial(_gdn_prep_kernel, c=c),
        grid=(bsz // tb_prep, t // tt),
        in_specs=[tile(tb_prep, C_QKV), tile(tb_prep, LANES, small_col), prev_spec,
                  pl.BlockSpec((tb_prep, keep, C_QKV), lambda b, i: (b, 0, 0)),
                  const((C_CONV, C_QKV)), const((1, LANES)), const((1, LANES))],
        out_specs=[tile(tb_prep, C_WIDTH), tile(tb_prep, C_WIDTH), tile(tb_prep, C_HEADS * c),
                   tile(tb_prep, C_WIDTH), tile(tb_prep, C_WIDTH), tile(tb_prep, LANES)],
        out_shape=[shape(C_WIDTH, F32), shape(C_WIDTH, BF16), shape(C_HEADS * c, BF16),
                   shape(C_WIDTH, BF16), shape(C_WIDTH, BF16), shape(LANES, F32)],
        scratch_shapes=[pltpu.VMEM((tb_prep, SUBLANES + tt, C_QKV), F32)],
        compiler_params=_cparams("parallel", "parallel"),
        name="gdn_prep",
    )(sc, sc, sc, buf, conv_w, pad_vec(a_log), pad_vec(dt_bias))
    st_spec = pl.BlockSpec((tb_scan, C_HEADS, C_DIM, C_DIM), lambda b, i: (b, 0, 0, 0))
    o, new_state = pl.pallas_call(
        functools.partial(_gdn_scan_kernel, c=c),
        grid=(bsz // tb_scan, t // tt),
        in_specs=[tile(tb_scan, C_WIDTH), tile(tb_scan, C_WIDTH), tile(tb_scan, C_HEADS * c),
                  tile(tb_scan, C_WIDTH), tile(tb_scan, C_WIDTH), tile(tb_scan, LANES),
                  tile(tb_scan, C_WIDTH, C_QKV // C_WIDTH), const((1, C_DIM)), st_spec],
        out_specs=[tile(tb_scan, C_WIDTH), st_spec],
        out_shape=[shape(C_WIDTH, BF16), jax.ShapeDtypeStruct((bsz, C_HEADS, C_DIM, C_DIM), F32)],
        scratch_shapes=[pltpu.VMEM((tb_scan, C_HEADS, C_DIM, C_DIM), F32)],
        compiler_params=_cparams("parallel", "arbitrary"),
        name="gdn_scan",
    )(sv, sk, qk, qt, kt, ec, sc, norm_w.reshape(1, C_DIM), state)
    return o, new_state, sc[:, t - keep:, 0:C_QKV]


def _merge_kernel(oa_ref, ob_ref, oc_ref, g_ref, x_ref, wa_ref, wb_ref, wc_ref, wo_ref,
                  lg_ref, lb_ref, y_ref, *, alpha):
    d = x_ref.shape[-1]
    mix = g_ref[:, 0:d].astype(F32) * _mm(oa_ref[...], wa_ref[...])
    mix = mix + g_ref[:, d:2 * d].astype(F32) * _mm(ob_ref[...], wb_ref[...])
    mix = mix + g_ref[:, 2 * d:3 * d].astype(F32) * _mm(oc_ref[...], wc_ref[...])
    y_ref[...] = _ln_rows(alpha * x_ref[...] + _mm(mix, wo_ref[...]), lg_ref[...], lb_ref[...])


def _merge(oa, ob, oc, gates, x, w_branch, wout, layer, ln_g, ln_b, alpha):
    n, d = x.shape
    tm = min(256, n)
    assert A_WIDTH == B_WIDTH == C_WIDTH
    row = lambda w: pl.BlockSpec((tm, w), lambda i: (i, 0))
    branch = lambda k: _resident(w_branch, layer, row_block=k, rows=A_WIDTH)
    return pl.pallas_call(
        functools.partial(_merge_kernel, alpha=alpha),
        grid=(n // tm,),
        in_specs=[row(A_WIDTH), row(B_WIDTH), row(C_WIDTH), row(N_BRANCH * d), row(d),
                  branch(0), branch(1), branch(2), _resident(wout, layer),
                  pl.BlockSpec((1, d), lambda i: (0, 0)), pl.BlockSpec((1, d), lambda i: (0, 0))],
        out_specs=row(d),
        out_shape=jax.ShapeDtypeStruct((n, d), F32),
        compiler_params=_cparams("parallel"),
        name="merge",
    )(oa, ob, oc, gates, x, w_branch, w_branch, w_branch, wout, ln_g.reshape(1, d), ln_b.reshape(1, d))


def _ffn_kernel(x_ref, b0_ref, wu_ref, cw_ref, cb_ref, wd_ref, lg_ref, lb_ref,
                yf_ref, yb_ref, b_ref, carry_ref, win_ref, act_ref, *, alpha):
    tb, tt, d = x_ref.shape
    d_ff = wd_ref.shape[0]
    m = tb * tt
    nj = d_ff // FFN_COLS
    keep = F_CONV - 1
    pad = SUBLANES
    ti = pl.program_id(1)

    @pl.when(ti == 0)
    def _():
        carry_ref[...] = b0_ref[...]

    x = x_ref[...].reshape(m, d)
    xb = x.astype(BF16)

    def up(c0):
        return jnp.dot(xb, wu_ref[:, c0:c0 + FFN_COLS], preferred_element_type=F32)

    def conv(slot, c0, rows):
        cols = slice(c0, c0 + FFN_COLS)
        win_ref[slot, :, 0:pad - keep, :] = jnp.zeros((tb, pad - keep, FFN_COLS), F32)
        win_ref[slot, :, pad - keep:pad, :] = carry_ref[:, :, cols]
        win_ref[slot, :, pad:pad + tt, :] = rows.reshape(tb, tt, FFN_COLS)
        win = win_ref[slot].reshape(tb * (pad + tt), FFN_COLS)
        out = win * cw_ref[keep:keep + 1, cols]
        for sft in range(1, F_CONV):
            out = out + _shift_rows(win, sft) * cw_ref[keep - sft:keep - sft + 1, cols]
        carry_ref[:, :, cols] = win_ref[slot, :, pad + tt - keep:pad + tt, :]
        return out.reshape(tb, pad + tt, FFN_COLS)[:, pad:, :] + cb_ref[:, cols]

    split = (nj + 1) // 2
    nxt = (up(0), up(d_ff))
    for j in range(nj):
        cur = nxt
        if j + 1 < nj:
            nxt = (up((j + 1) * FFN_COLS), up(d_ff + (j + 1) * FFN_COLS))
        gate = conv(2 * (j % 2), j * FFN_COLS, cur[0])
        val = conv(2 * (j % 2) + 1, d_ff + j * FFN_COLS, cur[1])
        act_ref[:, j * FFN_COLS:(j + 1) * FFN_COLS] = (_silu(gate) * val).reshape(m, FFN_COLS).astype(BF16)
        if j + 1 == split:
            down = jnp.dot(act_ref[:, 0:split * FFN_COLS], wd_ref[0:split * FFN_COLS, :],
                           preferred_element_type=F32)
    down = down + jnp.dot(act_ref[:, split * FFN_COLS:], wd_ref[split * FFN_COLS:, :],
                          preferred_element_type=F32)
    y = _ln_rows(alpha * x + down, lg_ref[...], lb_ref[...]).reshape(tb, tt, d)
    yf_ref[...] = y
    yb_ref[...] = y.astype(BF16)

    @pl.when(ti == pl.num_programs(1) - 1)
    def _():
        b_ref[...] = carry_ref[...]


def _ffn(x, buf, wu, cw, cb, wd, layer, ln_g, ln_b, alpha, tb, tt):
    bsz, t, d = x.shape
    d_ff = wd.shape[1]
    keep = F_CONV - 1
    row = pl.BlockSpec((tb, tt, d), lambda b, i: (b, i, 0))
    buf_spec = pl.BlockSpec((tb, keep, 2 * d_ff), lambda b, i: (b, 0, 0))
    const = lambda a: pl.BlockSpec(a.shape, lambda b, i: (0,) * a.ndim, pipeline_mode=pl.Buffered(1))
    lg = ln_g.reshape(1, d)
    lb = ln_b.reshape(1, d)
    cb = cb.reshape(1, 2 * d_ff)
    return pl.pallas_call(
        functools.partial(_ffn_kernel, alpha=alpha),
        grid=(bsz // tb, t // tt),
        in_specs=[row, buf_spec, _resident(wu, layer), const(cw), const(cb), _resident(wd, layer),
                  const(lg), const(lb)],
        out_specs=[row, row, buf_spec],
        out_shape=[jax.ShapeDtypeStruct((bsz, t, d), F32),
                   jax.ShapeDtypeStruct((bsz, t, d), BF16),
                   jax.ShapeDtypeStruct((bsz, keep, 2 * d_ff), F32)],
        scratch_shapes=[pltpu.VMEM((tb, keep, 2 * d_ff), F32),
                        pltpu.VMEM((4, tb, SUBLANES + tt, FFN_COLS), F32),
                        pltpu.VMEM((tb * tt, d_ff), BF16)],
        compiler_params=_cparams("parallel", "arbitrary"),
        name="conv_ffn",
    )(x, buf, wu, cw, cb, wd, lg, lb)


def _layer(xf, xb, state, p, prompt, alpha):
    k_cache, v_cache, s_hgrn, s_gdn, buf_gdn, buf_ffn = state
    bsz, t, d = xf.shape
    n = bsz * t
    x2 = xb.reshape(n, d)
    sq, skv, sb, sc, sg = _in_proj(x2, p["w_in"], p["layer"])
    sq, skv, sb, sc = (a.reshape(bsz, t, -1) for a in (sq, skv, sb, sc))
    oa, new_k, new_v = _swa(sq, skv, k_cache, v_cache, p["sink"], prompt)
    ob, new_hgrn = _hgrn(sb, p["lb"], p["hgrn_w"], s_hgrn, *((1, 256) if prompt else (8, t)))
    gdn_tiles = (1, bsz, 256) if prompt else (8, 8, t)
    oc, new_gdn, new_gconv = _gdn(sc, p["gdn_conv_w"], p["a_log"], p["dt_bias"], p["gdn_w"], s_gdn, buf_gdn,
                                  *gdn_tiles)
    y = _merge(oa.reshape(n, -1), ob.reshape(n, -1), oc.reshape(n, -1), sg, xf.reshape(n, d),
               p["w_branch"], p["wout"], p["layer"], p["ln1_g"], p["ln1_b"], alpha)
    tb, tt = (1, 256) if prompt else (8, t)
    zf, zb, new_fconv = _ffn(y.reshape(bsz, t, d), buf_ffn, p["wu"], p["fcw"], p["fcb"], p["wd"], p["layer"],
                             p["ln2_g"], p["ln2_b"], alpha, tb, tt)
    return zf, zb, (new_k, new_v, new_hgrn, new_gdn, new_gconv, new_fconv)


def kernel(x_prompt, x_sample, cache_swa_k, cache_swa_v, state_hgrn, state_gdn, state_gdn_conv, state_ffn_conv, ln_in_g, ln_in_b, w_in, attn_sinks, hgrn_lb_logits, hgrn_norm_w, gdn_conv_w, gdn_a_log, gdn_dt_bias, gdn_norm_w, w_branch, w_out, ln1_g, ln1_b, w_up, ffn_conv_w, ffn_conv_b, w_down, ln2_g, ln2_b):
    depth, d, _ = w_in.shape
    alpha = (2 * depth) ** 0.25
    lb_sm = jax.nn.softmax(hgrn_lb_logits.astype(F32), axis=0)
    lower_bounds = jnp.cumsum(lb_sm, axis=0) - lb_sm[0]

    o_a = A_WIDTH + 2 * A_KV_WIDTH
    o_b = o_a + 4 * B_WIDTH
    o_c = o_b + C_QKV + C_WIDTH
    o_s = o_c + 2 * C_HEADS
    wq = w_in[:, :, 0:A_WIDTH].astype(BF16)
    wkv = w_in[:, :, A_WIDTH:o_a].astype(BF16)
    wb = w_in[:, :, o_a:o_b].astype(BF16)
    wc = jnp.concatenate([w_in[:, :, o_b:o_s], jnp.zeros((depth, d, LANES - 2 * C_HEADS), w_in.dtype)],
                         axis=-1).astype(BF16)
    wg = w_in[:, :, o_s:].astype(BF16)
    w_in_groups = (wq, wkv, wb, wc, wg)
    w_branch_b = w_branch.astype(BF16)
    wout = w_out.astype(BF16)
    wu, fcw, fcb, wd = w_up.astype(BF16), ffn_conv_w.astype(F32), ffn_conv_b.astype(F32), w_down.astype(BF16)

    hp_f, hp_b = _layer_norm(x_prompt, ln_in_g, ln_in_b)
    hs_f, hs_b = _layer_norm(x_sample, ln_in_g, ln_in_b)
    bp = x_prompt.shape[0]
    prompt_state = (None, None,
                    jnp.zeros((bp, B_HEADS, B_DIM, B_DIM), F32),
                    jnp.zeros((bp, C_HEADS, C_DIM, C_DIM), F32),
                    jnp.zeros((bp, C_CONV - 1, C_QKV), F32),
                    jnp.zeros((bp, F_CONV - 1, w_up.shape[-1]), F32))
    p_new, s_new = [], []
    for l in range(depth):
        p = dict(layer=l, w_in=w_in_groups, sink=attn_sinks[l].astype(F32), lb=lower_bounds[l],
                 hgrn_w=hgrn_norm_w[l], gdn_conv_w=gdn_conv_w[l], a_log=gdn_a_log[l], dt_bias=gdn_dt_bias[l],
                 gdn_w=gdn_norm_w[l], w_branch=w_branch_b, wout=wout,
                 ln1_g=ln1_g[l], ln1_b=ln1_b[l], wu=wu, fcw=fcw[l], fcb=fcb[l], wd=wd,
                 ln2_g=ln2_g[l], ln2_b=ln2_b[l])
        hp_f, hp_b, st_p = _layer(hp_f, hp_b, prompt_state, p, True, alpha)
        sample_state = (cache_swa_k[l], cache_swa_v[l], state_hgrn[l], state_gdn[l],
                        state_gdn_conv[l], state_ffn_conv[l])
        hs_f, hs_b, st_s = _layer(hs_f, hs_b, sample_state, p, False, alpha)
        p_new.append(st_p)
        s_new.append(st_s)

    def stack(states, i):
        return jnp.stack([st[i] for st in states])

    return (hp_f, hs_f) + tuple(stack(p_new, i) for i in range(6)) + tuple(stack(s_new, i) for i in range(6))
```

```python
import functools
import math

import jax
import jax.numpy as jnp
from jax import lax
from jax.experimental import pallas as pl
from jax.experimental.pallas import tpu as pltpu

F32 = jnp.float32
BF16 = jnp.bfloat16

CHUNK = 64
WINDOW = 128
A_HEADS = 8
A_KV_HEADS = 2
A_GROUP = A_HEADS // A_KV_HEADS
A_HEAD_DIM = 64
A_SCALE = A_HEAD_DIM ** -0.5
A_WIDTH = A_HEADS * A_HEAD_DIM
A_KV_WIDTH = A_KV_HEADS * A_HEAD_DIM
B_HEADS = 4
B_DIM = 128
C_HEADS = 4
C_DIM = 128
C_CONV = 4
F_CONV = 3
N_BRANCH = 3
B_WIDTH = B_HEADS * B_DIM
C_WIDTH = C_HEADS * C_DIM
C_QKV = 3 * C_WIDTH
LN_EPS = 1e-5
RMS_EPS = 1e-6
NEG_BIG = -1e30
F_MIN = 1e-30

LANES = 128
SUBLANES = 8
FFN_COLS = 256
VMEM_LIMIT = 48 * 1024 * 1024


def _cparams(*sem):
    return pltpu.CompilerParams(dimension_semantics=sem, vmem_limit_bytes=VMEM_LIMIT)


def _resident(a, layer, row_block=None, rows=None):
    shape = (None, a.shape[1] if rows is None else rows, a.shape[2])
    rb = 0 if row_block is None else row_block
    return pl.BlockSpec(shape, lambda *_: (layer, rb, 0), pipeline_mode=pl.Buffered(1))


def _mm(a, b):
    return jnp.dot(a.astype(BF16), b.astype(BF16), preferred_element_type=F32)


def _mm_nt(a, b):
    return lax.dot_general(a.astype(BF16), b.astype(BF16), (((1,), (1,)), ((), ())),
                           preferred_element_type=F32)


def _mm_tn(a, b):
    return lax.dot_general(a.astype(BF16), b.astype(BF16), (((0,), (0,)), ((), ())),
                           preferred_element_type=F32)


def _mm_f32(a, b):
    return jnp.dot(a, b, preferred_element_type=F32, precision=lax.Precision.HIGHEST)


def _mm_nt_f32(a, b):
    return lax.dot_general(a, b, (((1,), (1,)), ((), ())), preferred_element_type=F32,
                           precision=lax.Precision.HIGHEST)


def _sigmoid(x):
    return 1.0 / (1.0 + jnp.exp(-x))


def _silu(x):
    return x * _sigmoid(x)


def _ln_rows(x, g, b):
    mu = jnp.mean(x, axis=-1, keepdims=True)
    xc = x - mu
    var = jnp.mean(xc * xc, axis=-1, keepdims=True)
    return xc * lax.rsqrt(var + LN_EPS) * g + b


def _rms_scaled(o, w):
    return o * lax.rsqrt(jnp.mean(o * o, axis=-1, keepdims=True) + RMS_EPS) * w


def _shift_rows(win, sft):
    rows, cols = win.shape
    groups = rows // SUBLANES
    rot = pltpu.roll(win.reshape(groups, SUBLANES, cols), sft, axis=1)
    above = jnp.concatenate([rot[groups - 1:], rot[:groups - 1]], axis=0)
    sub = lax.broadcasted_iota(jnp.int32, rot.shape, 1)
    return jnp.where(sub < sft, above, rot).reshape(rows, cols)


def _tri(c, lower):
    r = lax.broadcasted_iota(jnp.int32, (c, c), 0)
    s = lax.broadcasted_iota(jnp.int32, (c, c), 1)
    return jnp.where(r >= s if lower else r <= s, 1.0, 0.0).astype(F32)


def _ln_kernel(x_ref, g_ref, b_ref, of_ref, ob_ref):
    y = _ln_rows(x_ref[...], g_ref[...], b_ref[...])
    of_ref[...] = y
    ob_ref[...] = y.astype(BF16)


def _layer_norm(x, g, b):
    bsz, t, d = x.shape
    n = bsz * t
    tm = min(512, n)
    of, ob = pl.pallas_call(
        _ln_kernel,
        grid=(n // tm,),
        in_specs=[pl.BlockSpec((tm, d), lambda i: (i, 0)),
                  pl.BlockSpec((1, d), lambda i: (0, 0)),
                  pl.BlockSpec((1, d), lambda i: (0, 0))],
        out_specs=[pl.BlockSpec((tm, d), lambda i: (i, 0)),
                   pl.BlockSpec((tm, d), lambda i: (i, 0))],
        out_shape=[jax.ShapeDtypeStruct((n, d), F32), jax.ShapeDtypeStruct((n, d), BF16)],
        compiler_params=_cparams("parallel"),
        name="input_ln",
    )(x.reshape(n, d), g.reshape(1, d), b.reshape(1, d))
    return of.reshape(bsz, t, d), ob.reshape(bsz, t, d)


def _in_proj_kernel(x_ref, wq_ref, wkv_ref, wb_ref, wc_ref, wg_ref, q_ref, kv_ref, b_ref, c_ref, g_ref):
    x = x_ref[...]
    dot = lambda w_ref: jnp.dot(x, w_ref[...], preferred_element_type=F32)
    q_ref[...] = dot(wq_ref).astype(BF16)
    kv_ref[...] = dot(wkv_ref)
    hb = dot(wb_ref)
    b_ref[:, 0:B_WIDTH] = _silu(hb[:, 0:B_WIDTH])
    b_ref[:, B_WIDTH:2 * B_WIDTH] = _sigmoid(hb[:, B_WIDTH:2 * B_WIDTH])
    b_ref[:, 2 * B_WIDTH:3 * B_WIDTH] = hb[:, 2 * B_WIDTH:3 * B_WIDTH]
    b_ref[:, 3 * B_WIDTH:] = _silu(hb[:, 3 * B_WIDTH:])
    gc = dot(wc_ref)
    c_ref[:, 0:C_QKV] = gc[:, 0:C_QKV]
    c_ref[:, C_QKV:C_QKV + C_WIDTH] = _silu(gc[:, C_QKV:C_QKV + C_WIDTH])
    c_ref[:, C_QKV + C_WIDTH:] = gc[:, C_QKV + C_WIDTH:]
    g_ref[...] = _sigmoid(dot(wg_ref)).astype(BF16)


def _in_proj(x, ws, layer):
    n, k = x.shape
    tm = min(256, n)
    dts = (BF16, F32, F32, F32, BF16)
    return pl.pallas_call(
        _in_proj_kernel,
        grid=(n // tm,),
        in_specs=[pl.BlockSpec((tm, k), lambda i: (i, 0))] + [_resident(w, layer) for w in ws],
        out_specs=[pl.BlockSpec((tm, w.shape[2]), lambda i: (i, 0)) for w in ws],
        out_shape=[jax.ShapeDtypeStruct((n, w.shape[2]), dt) for w, dt in zip(ws, dts)],
        compiler_params=_cparams("parallel"),
        name="in_proj",
    )(x, *ws)


def _swa_kernel(q_ref, kvc_ref, kp_ref, vp_ref, sink_ref, o_ref, *cache_refs, prompt, qb, nsub):
    kvc = kvc_ref[...]
    kp = kp_ref[...]
    vp = vp_ref[...]
    nkeys = WINDOW + qb
    if prompt:
        row = lax.broadcasted_iota(jnp.int32, (qb, nkeys), 0)
        col = lax.broadcasted_iota(jnp.int32, (qb, nkeys), 1)
        behind = row // CHUNK - (col // CHUNK - WINDOW // CHUNK)
        band = (behind >= 0) & (behind <= WINDOW // CHUNK)
        first_valid = jnp.where(pl.program_id(1) == 0, WINDOW, 0)
        band_first = band & (col >= first_valid)
    cols = [slice(h * A_HEAD_DIM, (h + 1) * A_HEAD_DIM) for h in range(A_HEADS)]
    units = []
    for sub in range(nsub):
        rows = slice(sub * qb, (sub + 1) * qb)
        before = slice((sub - 1) * qb, sub * qb)
        for kv in range(A_KV_HEADS):
            ks = slice(kv * A_HEAD_DIM, (kv + 1) * A_HEAD_DIM)
            vs = slice(A_KV_WIDTH + kv * A_HEAD_DIM, A_KV_WIDTH + (kv + 1) * A_HEAD_DIM)
            k_prev, v_prev = (kp[:, ks], vp[:, ks]) if sub == 0 else (kvc[before, ks], kvc[before, vs])
            kk = jnp.concatenate([k_prev, kvc[rows, ks]], axis=0).astype(BF16)
            vv = jnp.concatenate([v_prev, kvc[rows, vs]], axis=0).astype(BF16)
            units += [(sub, kv * A_GROUP + g, rows, kk, vv) for g in range(A_GROUP)]
    scores = [_mm_nt(q_ref[rows, cols[h]], kk) for _, h, rows, kk, _ in units]
    es, dens = [], []
    for (sub, h, _, _, _), s in zip(units, scores):
        if prompt:
            s = jnp.where(band_first if sub == 0 else band, s, NEG_BIG)
        sink = sink_ref[h]
        m = jnp.maximum(jnp.max(s, axis=-1, keepdims=True), sink)
        e = jnp.exp(s - m)
        es.append(e.astype(BF16))
        dens.append(jnp.sum(e, axis=-1, keepdims=True) + jnp.exp(sink - m))
    outs = [_mm(e, vv) for e, (_, _, _, _, vv) in zip(es, units)]
    for (_, h, rows, _, _), o, den in zip(units, outs, dens):
        o_ref[rows, cols[h]] = (o / den).astype(o_ref.dtype)
    if not prompt:
        nk_ref, nv_ref = cache_refs
        keep = WINDOW - qb
        nk_ref[0:keep, :] = kp[qb:, :]
        nk_ref[keep:, :] = kvc[:, 0:A_KV_WIDTH]
        nv_ref[0:keep, :] = vp[qb:, :]
        nv_ref[keep:, :] = kvc[:, A_KV_WIDTH:]


def _swa(sq, skv, k_cache, v_cache, sink, prompt):
    bsz, t, _ = sq.shape
    if prompt:
        qb = WINDOW
        nsub = 2 if t % (2 * qb) == 0 else 1
        assert t % (nsub * qb) == 0
        kp_spec = pl.BlockSpec((None, WINDOW, A_KV_WIDTH), lambda b, i: (b, jnp.maximum(i * nsub - 1, 0), 0))
        vp_spec = pl.BlockSpec((None, WINDOW, A_KV_WIDTH), lambda b, i: (b, jnp.maximum(i * nsub - 1, 0), 1))
        kp_arr, vp_arr = skv, skv
        out_shape = [jax.ShapeDtypeStruct((bsz, t, A_WIDTH), BF16)]
        out_specs = [pl.BlockSpec((None, nsub * qb, A_WIDTH), lambda b, i: (b, i, 0))]
    else:
        qb = t
        nsub = 1
        assert t <= WINDOW and t % (2 * SUBLANES) == 0 and k_cache.shape[1] == WINDOW
        kp_spec = pl.BlockSpec((None, WINDOW, A_KV_WIDTH), lambda b, i: (b, 0, 0))
        vp_spec = kp_spec
        kp_arr = k_cache.reshape(bsz, WINDOW, A_KV_WIDTH)
        vp_arr = v_cache.reshape(bsz, WINDOW, A_KV_WIDTH)
        out_shape = [jax.ShapeDtypeStruct((bsz, t, A_WIDTH), BF16),
                     jax.ShapeDtypeStruct((bsz, WINDOW, A_KV_WIDTH), F32),
                     jax.ShapeDtypeStruct((bsz, WINDOW, A_KV_WIDTH), F32)]
        cache_spec = pl.BlockSpec((None, WINDOW, A_KV_WIDTH), lambda b, i: (b, 0, 0))
        out_specs = [pl.BlockSpec((None, qb, A_WIDTH), lambda b, i: (b, i, 0)), cache_spec, cache_spec]
    outs = pl.pallas_call(
        functools.partial(_swa_kernel, prompt=prompt, qb=qb, nsub=nsub),
        grid=(bsz, t // (nsub * qb)),
        in_specs=[pl.BlockSpec((None, nsub * qb, A_WIDTH), lambda b, i: (b, i, 0)),
                  pl.BlockSpec((None, nsub * qb, 2 * A_KV_WIDTH), lambda b, i: (b, i, 0)),
                  kp_spec, vp_spec,
                  pl.BlockSpec(memory_space=pltpu.SMEM)],
        out_specs=out_specs,
        out_shape=out_shape,
        compiler_params=_cparams("parallel", "parallel"),
        name="swa",
    )(sq, skv, kp_arr, vp_arr, sink)
    if prompt:
        new_k = skv[:, t - WINDOW:, 0:A_KV_WIDTH]
        new_v = skv[:, t - WINDOW:, A_KV_WIDTH:]
    else:
        new_k, new_v = outs[1], outs[2]
    shape = (bsz, WINDOW, A_KV_HEADS, A_HEAD_DIM)
    return outs[0], new_k.reshape(shape), new_v.reshape(shape)


def _hgrn_kernel(x_ref, lb_ref, w_ref, s0_ref, o_ref, s_ref, st_ref, *, c):
    tb, tt, _ = x_ref.shape
    ti = pl.program_id(1)

    @pl.when(ti == 0)
    def _():
        for b in range(tb):
            for h in range(B_HEADS):
                st_ref[b, h] = s0_ref[b, h].T

    pos = lax.broadcasted_iota(jnp.int32, (tt, B_DIM), 0)
    r = lax.broadcasted_iota(jnp.int32, (c, c), 0)
    s = lax.broadcasted_iota(jnp.int32, (c, c), 1)
    x = r ^ s
    sizes = [1 << i for i in range(int(math.log2(c)))]
    levels = {hh: (r > s) & (x >= hh) & (x < 2 * hh) for hh in sizes}
    uppers = {hh: (pos % (2 * hh)) >= hh for hh in sizes}
    heads = []
    for b in range(tb):
        for h in range(B_HEADS):
            cs = slice(h * B_DIM, (h + 1) * B_DIM)
            q = x_ref[b, :, cs]
            sig = x_ref[b, :, B_WIDTH + h * B_DIM:B_WIDTH + (h + 1) * B_DIM]
            v = x_ref[b, :, 2 * B_WIDTH + h * B_DIM:2 * B_WIDTH + (h + 1) * B_DIM]
            gate = x_ref[b, :, 3 * B_WIDTH + h * B_DIM:3 * B_WIDTH + (h + 1) * B_DIM]
            lb = lb_ref[:, cs]
            f = jnp.maximum(lb + (1.0 - lb) * sig, F_MIN)
            k = (1.0 - lb) * (1.0 - sig)
            total = f
            qd = q * f
            kd = k
            att = [jnp.where(r == s, jnp.sum((q * k)[j * c:(j + 1) * c], axis=-1, keepdims=True), 0.0)
                   for j in range(tt // c)]
            hh = 1
            while hh < c:
                qh = qd.astype(BF16)
                kh = kd.astype(BF16)
                for j in range(tt // c):
                    rows = slice(j * c, (j + 1) * c)
                    att[j] = jnp.where(levels[hh], _mm_nt(qh[rows], kh[rows]), att[j])
                below = pltpu.roll(total, hh, axis=0)
                above = pltpu.roll(total, tt - hh, axis=0)
                qd = jnp.where(uppers[hh], qd * below, qd)
                kd = jnp.where(uppers[hh], kd, kd * above)
                total = total * jnp.where(uppers[hh], below, above)
                hh *= 2
            heads.append(dict(b=b, h=h, cs=cs, att=att, qc=qd.astype(BF16), kc=kd.astype(BF16), v=v.astype(BF16),
                              gate=gate, total=total))
    chunks = [slice(j * c, (j + 1) * c) for j in range(tt // c)]
    for hd in heads:
        hd["av"] = [_mm(hd["att"][j], hd["v"][rows]) for j, rows in enumerate(chunks)]
        hd["kv"] = [_mm_tn(hd["v"][rows], hd["kc"][rows]) for rows in chunks]
    for j, rows in enumerate(chunks):
        for hd in heads:
            b, h = hd["b"], hd["h"]
            st = st_ref[b, h]
            o = _mm_nt(hd["qc"][rows], st) + hd["av"][j]
            st_ref[b, h] = st * hd["total"][(j + 1) * c - 1:(j + 1) * c, :] + hd["kv"][j]
            o_ref[b, rows, hd["cs"]] = (_rms_scaled(o, w_ref[...]) * hd["gate"][rows]).astype(o_ref.dtype)

    @pl.when(ti == pl.num_programs(1) - 1)
    def _():
        for b in range(tb):
            for h in range(B_HEADS):
                s_ref[b, h] = st_ref[b, h].T


def _hgrn(sb, lb, norm_w, state, tb, tt):
    bsz, t, _ = sb.shape
    c = min(CHUNK, t)
    assert tt % c == 0 and t % tt == 0 and bsz % tb == 0
    st_spec = pl.BlockSpec((tb, B_HEADS, B_DIM, B_DIM), lambda b, i: (b, 0, 0, 0))
    return pl.pallas_call(
        functools.partial(_hgrn_kernel, c=c),
        grid=(bsz // tb, t // tt),
        in_specs=[pl.BlockSpec((tb, tt, 4 * B_WIDTH), lambda b, i: (b, i, 0)),
                  pl.BlockSpec((1, B_WIDTH), lambda b, i: (0, 0)),
                  pl.BlockSpec((1, B_DIM), lambda b, i: (0, 0)),
                  st_spec],
        out_specs=[pl.BlockSpec((tb, tt, B_WIDTH), lambda b, i: (b, i, 0)), st_spec],
        out_shape=[jax.ShapeDtypeStruct((bsz, t, B_WIDTH), BF16),
                   jax.ShapeDtypeStruct((bsz, B_HEADS, B_DIM, B_DIM), F32)],
        scratch_shapes=[pltpu.VMEM((tb, B_HEADS, B_DIM, B_DIM), F32)],
        compiler_params=_cparams("parallel", "arbitrary"),
        name="hgrn",
    )(sb, lb.reshape(1, B_WIDTH), norm_w.reshape(1, B_DIM), state)


def _gdn_prep_kernel(x_ref, sm_ref, prev_ref, b0_ref, cw_ref, alog_ref, dtb_ref,
                     sv_ref, sk_ref, qk_ref, qt_ref, kt_ref, ec_ref, cbuf_ref, *, c):
    tb, tt, _ = x_ref.shape
    pad = SUBLANES
    keep = C_CONV - 1
    ti = pl.program_id(1)

    @pl.when(ti == 0)
    def _():
        cbuf_ref[:, pad - keep:pad, :] = b0_ref[...]

    @pl.when(ti > 0)
    def _():
        cbuf_ref[:, pad - keep:pad, :] = prev_ref[:, pad - keep:pad, :]

    cbuf_ref[:, 0:pad - keep, :] = jnp.zeros((tb, pad - keep, C_QKV), F32)
    cbuf_ref[:, pad:pad + tt, :] = x_ref[...]
    win = cbuf_ref[...].reshape(tb * (pad + tt), C_QKV)
    u = win * cw_ref[keep:keep + 1, :]
    for sft in range(1, C_CONV):
        u = u + _shift_rows(win, sft) * cw_ref[keep - sft:keep - sft + 1, :]
    u = _silu(u.reshape(tb, pad + tt, C_QKV)[:, pad:, :])

    small = sm_ref[...]
    pre = small + dtb_ref[...]
    softplus = jnp.maximum(pre, 0.0) + jnp.log(1.0 + jnp.exp(-jnp.abs(pre)))
    g_all = -jnp.exp(alog_ref[...]) * softplus
    beta_all = _sigmoid(small)
    tr = lax.broadcasted_iota(jnp.int32, (tt, tt), 0)
    ts = lax.broadcasted_iota(jnp.int32, (tt, tt), 1)
    chunk_tri = jnp.where((tr >= ts) & (tr // c == ts // c), 1.0, 0.0).astype(F32)
    sel_r = lax.broadcasted_iota(jnp.int32, (SUBLANES, LANES), 0)
    sel_c = lax.broadcasted_iota(jnp.int32, (SUBLANES, LANES), 1)
    sel = jnp.where(sel_r == sel_c, 1.0, 0.0).astype(F32)
    r = lax.broadcasted_iota(jnp.int32, (c, c), 0)
    s = lax.broadcasted_iota(jnp.int32, (c, c), 1)
    incl = r >= s
    units = []
    for b in range(tb):
        cum_all = _mm_f32(chunk_tri, g_all[b])
        cum_rows = _mm_nt_f32(sel, cum_all)
        ecum_all = jnp.exp(cum_all)
        ec_ref[b] = ecum_all
        for h in range(C_HEADS):
            cs = slice(h * C_DIM, (h + 1) * C_DIM)
            uq = u[b, :, cs]
            uk = u[b, :, C_WIDTH + h * C_DIM:C_WIDTH + (h + 1) * C_DIM]
            v = u[b, :, 2 * C_WIDTH + h * C_DIM:2 * C_WIDTH + (h + 1) * C_DIM]
            q = uq * lax.rsqrt(jnp.sum(uq * uq, axis=-1, keepdims=True) + 1e-6) * (C_DIM ** -0.5)
            k = uk * lax.rsqrt(jnp.sum(uk * uk, axis=-1, keepdims=True) + 1e-6)
            beta = beta_all[b, :, h:h + 1]
            cum = cum_all[:, C_HEADS + h:C_HEADS + h + 1]
            ecum = ecum_all[:, C_HEADS + h:C_HEADS + h + 1]
            qt_ref[b, :, cs] = (q * ecum).astype(BF16)
            rhs = jnp.concatenate([v * beta, k * (beta * ecum)], axis=-1)
            for j in range(tt // c):
                rows = slice(j * c, (j + 1) * c)
                cum_j = cum[rows]
                cum_row = cum_rows[C_HEADS + h:C_HEADS + h + 1, rows]
                dec = jnp.where(incl, jnp.exp(jnp.where(incl, cum_j - cum_row, 0.0)), 0.0)
                k_j = k[rows]
                kb = k_j.astype(BF16)
                last = cum_j[c - 1:c, :]
                kt_ref[b, rows, cs] = (k_j * jnp.exp(last - cum_j)).astype(BF16)
                units.append(dict(b=b, rows=rows, cs=cs, qcols=slice(h * c, (h + 1) * c), dec=dec, kb=kb,
                                  qb=q[rows].astype(BF16), beta=beta[rows], rhs=rhs[rows]))
    for un in units:
        un["kk"] = _mm_nt(un["kb"], un["kb"])
        un["qk"] = _mm_nt(un["qb"], un["kb"])
    for un in units:
        qk_ref[un["b"], un["rows"], un["qcols"]] = (un["qk"] * un["dec"]).astype(BF16)
        n = jnp.where(r > s, un["beta"] * un["kk"] * un["dec"], 0.0)
        un["p"] = n
        un["corr"] = -n
    for _ in range(int(math.log2(c)) - 1):
        for un in units:
            pb = un["p"].astype(BF16)
            un["p"] = jnp.dot(pb, pb, preferred_element_type=F32)
        for un in units:
            un["corr"] = un["corr"] + un["p"] + _mm(un["corr"], un["p"])
    for un in units:
        un["sol"] = un["rhs"] + _mm(un["corr"], un["rhs"])
    for un in units:
        sv_ref[un["b"], un["rows"], un["cs"]] = un["sol"][:, 0:C_DIM]
        sk_ref[un["b"], un["rows"], un["cs"]] = un["sol"][:, C_DIM:].astype(BF16)


def _gdn_scan_kernel(sv_ref, sk_ref, qk_ref, qt_ref, kt_ref, ec_ref, z_ref, w_ref, s0_ref,
                     o_ref, s_ref, st_ref, *, c):
    tb, tt, _ = sv_ref.shape
    ti = pl.program_id(1)

    @pl.when(ti == 0)
    def _():
        st_ref[...] = s0_ref[...]

    heads = [(b, h, slice(h * C_DIM, (h + 1) * C_DIM)) for b in range(tb) for h in range(C_HEADS)]
    for j in range(tt // c):
        rows = slice(j * c, (j + 1) * c)
        st = [st_ref[b, h] for b, h, _ in heads]
        stb = [a.astype(BF16) for a in st]
        sks = [_mm(sk_ref[b, rows, cs], stb[i]) for i, (b, h, cs) in enumerate(heads)]
        qs = [_mm(qt_ref[b, rows, cs], stb[i]) for i, (b, h, cs) in enumerate(heads)]
        wb = [(sv_ref[b, rows, cs] - sks[i]).astype(BF16) for i, (b, h, cs) in enumerate(heads)]
        upd = [_mm_tn(kt_ref[b, rows, cs], wb[i]) for i, (b, h, cs) in enumerate(heads)]
        qkw = [_mm(qk_ref[b, rows, h * c:(h + 1) * c], wb[i]) for i, (b, h, cs) in enumerate(heads)]
        for i, (b, h, cs) in enumerate(heads):
            e_last = ec_ref[b, (j + 1) * c - 1:(j + 1) * c, C_HEADS + h:C_HEADS + h + 1]
            st_ref[b, h] = e_last * st[i] + upd[i]
            o_ref[b, rows, cs] = (_rms_scaled(qs[i] + qkw[i], w_ref[...]) * z_ref[b, rows, cs]).astype(o_ref.dtype)

    @pl.when(ti == pl.num_programs(1) - 1)
    def _():
        s_ref[...] = st_ref[...]


def _gdn(sc, conv_w, a_log, dt_bias, norm_w, state, buf, tb_prep, tb_scan, tt):
    bsz, t, _ = sc.shape
    c = min(CHUNK, t)
    keep = C_CONV - 1
    assert t >= keep and tt % c == 0 and t % tt == 0
    pad_vec = lambda a: jnp.zeros((1, LANES), F32).at[0, C_HEADS:2 * C_HEADS].set(a.astype(F32))
    const = lambda shape: pl.BlockSpec(shape, lambda b, i: (0,) * len(shape))
    tile = lambda tb, w, col=0: pl.BlockSpec((tb, tt, w), lambda b, i: (b, i, col))
    shape = lambda w, dt: jax.ShapeDtypeStruct((bsz, t, w), dt)
    small_col = (C_QKV + C_WIDTH) // LANES
    prev_spec = pl.BlockSpec((tb_prep, SUBLANES, C_QKV),
                             lambda b, i: (b, jnp.maximum(i * (tt // SUBLANES) - 1, 0), 0))
    sv, sk, qk, qt, kt, ec = pl.pallas_call(
        functools.partial(_gdn_prep_kernel, c=c),
        grid=(bsz // tb_prep, t // tt),
        in_specs=[tile(tb_prep, C_QKV), tile(tb_prep, LANES, small_col), prev_spec,
                  pl.BlockSpec((tb_prep, keep, C_QKV), lambda b, i: (b, 0, 0)),
                  const((C_CONV, C_QKV)), const((1, LANES)), const((1, LANES))],
        out_specs=[tile(tb_prep, C_WIDTH), tile(tb_prep, C_WIDTH), tile(tb_prep, C_HEADS * c),
                   tile(tb_prep, C_WIDTH), tile(tb_prep, C_WIDTH), tile(tb_prep, LANES)],
        out_shape=[shape(C_WIDTH, F32), shape(C_WIDTH, BF16), shape(C_HEADS * c, BF16),
                   shape(C_WIDTH, BF16), shape(C_WIDTH, BF16), shape(LANES, F32)],
        scratch_shapes=[pltpu.VMEM((tb_prep, SUBLANES + tt, C_QKV), F32)],
        compiler_params=_cparams("parallel", "parallel"),
        name="gdn_prep",
    )(sc, sc, sc, buf, conv_w, pad_vec(a_log), pad_vec(dt_bias))
    st_spec = pl.BlockSpec((tb_scan, C_HEADS, C_DIM, C_DIM), lambda b, i: (b, 0, 0, 0))
    o, new_state = pl.pallas_call(
        functools.partial(_gdn_scan_kernel, c=c),
        grid=(bsz // tb_scan, t // tt),
        in_specs=[tile(tb_scan, C_WIDTH), tile(tb_scan, C_WIDTH), tile(tb_scan, C_HEADS * c),
                  tile(tb_scan, C_WIDTH), tile(tb_scan, C_WIDTH), tile(tb_scan, LANES),
                  tile(tb_scan, C_WIDTH, C_QKV // C_WIDTH), const((1, C_DIM)), st_spec],
        out_specs=[tile(tb_scan, C_WIDTH), st_spec],
        out_shape=[shape(C_WIDTH, BF16), jax.ShapeDtypeStruct((bsz, C_HEADS, C_DIM, C_DIM), F32)],
        scratch_shapes=[pltpu.VMEM((tb_scan, C_HEADS, C_DIM, C_DIM), F32)],
        compiler_params=_cparams("parallel", "arbitrary"),
        name="gdn_scan",
    )(sv, sk, qk, qt, kt, ec, sc, norm_w.reshape(1, C_DIM), state)
    return o, new_state, sc[:, t - keep:, 0:C_QKV]


def _merge_kernel(oa_ref, ob_ref, oc_ref, g_ref, x_ref, wa_ref, wb_ref, wc_ref, wo_ref,
                  lg_ref, lb_ref, y_ref, *, alpha):
    d = x_ref.shape[-1]
    mix = g_ref[:, 0:d].astype(F32) * _mm(oa_ref[...], wa_ref[...])
    mix = mix + g_ref[:, d:2 * d].astype(F32) * _mm(ob_ref[...], wb_ref[...])
    mix = mix + g_ref[:, 2 * d:3 * d].astype(F32) * _mm(oc_ref[...], wc_ref[...])
    y_ref[...] = _ln_rows(alpha * x_ref[...] + _mm(mix, wo_ref[...]), lg_ref[...], lb_ref[...])


def _merge(oa, ob, oc, gates, x, w_branch, wout, layer, ln_g, ln_b, alpha):
    n, d = x.shape
    tm = min(256, n)
    assert A_WIDTH == B_WIDTH == C_WIDTH
    row = lambda w: pl.BlockSpec((tm, w), lambda i: (i, 0))
    branch = lambda k: _resident(w_branch, layer, row_block=k, rows=A_WIDTH)
    return pl.pallas_call(
        functools.partial(_merge_kernel, alpha=alpha),
        grid=(n // tm,),
        in_specs=[row(A_WIDTH), row(B_WIDTH), row(C_WIDTH), row(N_BRANCH * d), row(d),
                  branch(0), branch(1), branch(2), _resident(wout, layer),
                  pl.BlockSpec((1, d), lambda i: (0, 0)), pl.BlockSpec((1, d), lambda i: (0, 0))],
        out_specs=row(d),
        out_shape=jax.ShapeDtypeStruct((n, d), F32),
        compiler_params=_cparams("parallel"),
        name="merge",
    )(oa, ob, oc, gates, x, w_branch, w_branch, w_branch, wout, ln_g.reshape(1, d), ln_b.reshape(1, d))


def _ffn_kernel(x_ref, b0_ref, wu_ref, cw_ref, cb_ref, wd_ref, lg_ref, lb_ref,
                yf_ref, yb_ref, b_ref, carry_ref, act_ref, *, alpha):
    tb, tt, d = x_ref.shape
    d_ff = wd_ref.shape[0]
    m = tb * tt
    nj = d_ff // FFN_COLS
    keep = F_CONV - 1
    pad = SUBLANES
    ti = pl.program_id(1)

    @pl.when(ti == 0)
    def _():
        carry_ref[:, 0:pad - keep, :] = jnp.zeros((tb, pad - keep, carry_ref.shape[-1]), F32)
        carry_ref[:, pad - keep:, :] = b0_ref[...]

    x = x_ref[...].reshape(m, d)
    xb = x.astype(BF16)

    def up(c0):
        return jnp.dot(xb, wu_ref[:, c0:c0 + FFN_COLS], preferred_element_type=F32)

    def conv(c0, rows):
        cols = slice(c0, c0 + FFN_COLS)
        rows = rows.reshape(tb, tt, FFN_COLS)
        win = jnp.concatenate([carry_ref[:, :, cols], rows], axis=1).reshape(tb * (pad + tt), FFN_COLS)
        out = win * cw_ref[keep:keep + 1, cols]
        for sft in range(1, F_CONV):
            out = out + _shift_rows(win, sft) * cw_ref[keep - sft:keep - sft + 1, cols]
        carry_ref[:, pad - keep:, cols] = rows[:, tt - keep:, :]
        return out.reshape(tb, pad + tt, FFN_COLS)[:, pad:, :] + cb_ref[:, cols]

    parts = [0, (nj + 2) // 3, (2 * nj + 2) // 3, nj]
    down = None
    nxt = (up(0), up(d_ff))
    for j in range(nj):
        cur = nxt
        if j + 1 < nj:
            nxt = (up((j + 1) * FFN_COLS), up(d_ff + (j + 1) * FFN_COLS))
        gate = conv(j * FFN_COLS, cur[0])
        val = conv(d_ff + j * FFN_COLS, cur[1])
        act_ref[:, j * FFN_COLS:(j + 1) * FFN_COLS] = (_silu(gate) * val).reshape(m, FFN_COLS).astype(BF16)
        if j + 1 in parts:
            lo = parts[parts.index(j + 1) - 1] * FFN_COLS
            part = jnp.dot(act_ref[:, lo:(j + 1) * FFN_COLS], wd_ref[lo:(j + 1) * FFN_COLS, :],
                           preferred_element_type=F32)
            down = part if down is None else down + part
    y = _ln_rows(alpha * x + down, lg_ref[...], lb_ref[...]).reshape(tb, tt, d)
    yf_ref[...] = y
    yb_ref[...] = y.astype(BF16)

    @pl.when(ti == pl.num_programs(1) - 1)
    def _():
        b_ref[...] = carry_ref[:, pad - keep:, :]


def _ffn(x, buf, wu, cw, cb, wd, layer, ln_g, ln_b, alpha, tb, tt):
    bsz, t, d = x.shape
    d_ff = wd.shape[1]
    keep = F_CONV - 1
    row = pl.BlockSpec((tb, tt, d), lambda b, i: (b, i, 0))
    buf_spec = pl.BlockSpec((tb, keep, 2 * d_ff), lambda b, i: (b, 0, 0))
    const = lambda a: pl.BlockSpec(a.shape, lambda b, i: (0,) * a.ndim, pipeline_mode=pl.Buffered(1))
    lg = ln_g.reshape(1, d)
    lb = ln_b.reshape(1, d)
    cb = cb.reshape(1, 2 * d_ff)
    return pl.pallas_call(
        functools.partial(_ffn_kernel, alpha=alpha),
        grid=(bsz // tb, t // tt),
        in_specs=[row, buf_spec, _resident(wu, layer), const(cw), const(cb), _resident(wd, layer),
                  const(lg), const(lb)],
        out_specs=[row, row, buf_spec],
        out_shape=[jax.ShapeDtypeStruct((bsz, t, d), F32),
                   jax.ShapeDtypeStruct((bsz, t, d), BF16),
                   jax.ShapeDtypeStruct((bsz, keep, 2 * d_ff), F32)],
        scratch_shapes=[pltpu.VMEM((tb, SUBLANES, 2 * d_ff), F32),
                        pltpu.VMEM((tb * tt, d_ff), BF16)],
        compiler_params=_cparams("parallel", "arbitrary"),
        name="conv_ffn",
    )(x, buf, wu, cw, cb, wd, lg, lb)


def _layer(xf, xb, state, p, prompt, alpha):
    k_cache, v_cache, s_hgrn, s_gdn, buf_gdn, buf_ffn = state
    bsz, t, d = xf.shape
    n = bsz * t
    x2 = xb.reshape(n, d)
    sq, skv, sb, sc, sg = _in_proj(x2, p["w_in"], p["layer"])
    sq, skv, sb, sc = (a.reshape(bsz, t, -1) for a in (sq, skv, sb, sc))
    oa, new_k, new_v = _swa(sq, skv, k_cache, v_cache, p["sink"], prompt)
    ob, new_hgrn = _hgrn(sb, p["lb"], p["hgrn_w"], s_hgrn, *((1, 256) if prompt else (8, t)))
    gdn_tiles = (1, bsz, 256) if prompt else (8, 8, t)
    oc, new_gdn, new_gconv = _gdn(sc, p["gdn_conv_w"], p["a_log"], p["dt_bias"], p["gdn_w"], s_gdn, buf_gdn,
                                  *gdn_tiles)
    y = _merge(oa.reshape(n, -1), ob.reshape(n, -1), oc.reshape(n, -1), sg, xf.reshape(n, d),
               p["w_branch"], p["wout"], p["layer"], p["ln1_g"], p["ln1_b"], alpha)
    tb, tt = (1, 256) if prompt else (8, t)
    zf, zb, new_fconv = _ffn(y.reshape(bsz, t, d), buf_ffn, p["wu"], p["fcw"], p["fcb"], p["wd"], p["layer"],
                             p["ln2_g"], p["ln2_b"], alpha, tb, tt)
    return zf, zb, (new_k, new_v, new_hgrn, new_gdn, new_gconv, new_fconv)


def kernel(x_prompt, x_sample, cache_swa_k, cache_swa_v, state_hgrn, state_gdn, state_gdn_conv, state_ffn_conv, ln_in_g, ln_in_b, w_in, attn_sinks, hgrn_lb_logits, hgrn_norm_w, gdn_conv_w, gdn_a_log, gdn_dt_bias, gdn_norm_w, w_branch, w_out, ln1_g, ln1_b, w_up, ffn_conv_w, ffn_conv_b, w_down, ln2_g, ln2_b):
    depth, d, _ = w_in.shape
    alpha = (2 * depth) ** 0.25
    lb_sm = jax.nn.softmax(hgrn_lb_logits.astype(F32), axis=0)
    lower_bounds = jnp.cumsum(lb_sm, axis=0) - lb_sm[0]

    o_a = A_WIDTH + 2 * A_KV_WIDTH
    o_b = o_a + 4 * B_WIDTH
    o_c = o_b + C_QKV + C_WIDTH
    o_s = o_c + 2 * C_HEADS
    assert math.log2(A_HEAD_DIM) % 2 == 0
    wq = (w_in[:, :, 0:A_WIDTH] * A_SCALE).astype(BF16)
    wkv = w_in[:, :, A_WIDTH:o_a].astype(BF16)
    wb = w_in[:, :, o_a:o_b].astype(BF16)
    wc = jnp.concatenate([w_in[:, :, o_b:o_s], jnp.zeros((depth, d, LANES - 2 * C_HEADS), w_in.dtype)],
                         axis=-1).astype(BF16)
    wg = w_in[:, :, o_s:].astype(BF16)
    w_in_groups = (wq, wkv, wb, wc, wg)
    w_branch_b = w_branch.astype(BF16)
    wout = w_out.astype(BF16)
    wu, fcw, fcb, wd = w_up.astype(BF16), ffn_conv_w.astype(F32), ffn_conv_b.astype(F32), w_down.astype(BF16)

    hp_f, hp_b = _layer_norm(x_prompt, ln_in_g, ln_in_b)
    hs_f, hs_b = _layer_norm(x_sample, ln_in_g, ln_in_b)
    bp = x_prompt.shape[0]
    prompt_state = (None, None,
                    jnp.zeros((bp, B_HEADS, B_DIM, B_DIM), F32),
                    jnp.zeros((bp, C_HEADS, C_DIM, C_DIM), F32),
                    jnp.zeros((bp, C_CONV - 1, C_QKV), F32),
                    jnp.zeros((bp, F_CONV - 1, w_up.shape[-1]), F32))
    p_new, s_new = [], []
    for l in range(depth):
        p = dict(layer=l, w_in=w_in_groups, sink=attn_sinks[l].astype(F32), lb=lower_bounds[l],
                 hgrn_w=hgrn_norm_w[l], gdn_conv_w=gdn_conv_w[l], a_log=gdn_a_log[l], dt_bias=gdn_dt_bias[l],
                 gdn_w=gdn_norm_w[l], w_branch=w_branch_b, wout=wout,
                 ln1_g=ln1_g[l], ln1_b=ln1_b[l], wu=wu, fcw=fcw[l], fcb=fcb[l], wd=wd,
                 ln2_g=ln2_g[l], ln2_b=ln2_b[l])
        hp_f, hp_b, st_p = _layer(hp_f, hp_b, prompt_state, p, True, alpha)
        sample_state = (cache_swa_k[l], cache_swa_v[l], state_hgrn[l], state_gdn[l],
                        state_gdn_conv[l], state_ffn_conv[l])
        hs_f, hs_b, st_s = _layer(hs_f, hs_b, sample_state, p, False, alpha)
        p_new.append(st_p)
        s_new.append(st_s)

    def stack(states, i):
        return jnp.stack([st[i] for st in states])

    return (hp_f, hs_f) + tuple(stack(p_new, i) for i in range(6)) + tuple(stack(s_new, i) for i in range(6))
```

```python
import functools
import math

import jax
import jax.numpy as jnp
from jax import lax
from jax.experimental import pallas as pl
from jax.experimental.pallas import tpu as pltpu

F32 = jnp.float32
BF16 = jnp.bfloat16

CHUNK = 64
WINDOW = 128
A_HEADS = 8
A_KV_HEADS = 2
A_GROUP = A_HEADS // A_KV_HEADS
A_HEAD_DIM = 64
A_SCALE = A_HEAD_DIM ** -0.5
A_WIDTH = A_HEADS * A_HEAD_DIM
A_KV_WIDTH = A_KV_HEADS * A_HEAD_DIM
B_HEADS = 4
B_DIM = 128
C_HEADS = 4
C_DIM = 128
C_CONV = 4
F_CONV = 3
N_BRANCH = 3
B_WIDTH = B_HEADS * B_DIM
C_WIDTH = C_HEADS * C_DIM
C_QKV = 3 * C_WIDTH
LN_EPS = 1e-5
RMS_EPS = 1e-6
NEG_BIG = -1e30
F_MIN = 1e-30

LANES = 128
SUBLANES = 8
FFN_COLS = 256
VMEM_LIMIT = 48 * 1024 * 1024


def _cparams(*sem):
    return pltpu.CompilerParams(dimension_semantics=sem, vmem_limit_bytes=VMEM_LIMIT)


def _resident(a, layer, row_block=None, rows=None):
    shape = (None, a.shape[1] if rows is None else rows, a.shape[2])
    rb = 0 if row_block is None else row_block
    return pl.BlockSpec(shape, lambda *_: (layer, rb, 0), pipeline_mode=pl.Buffered(1))


def _mm(a, b):
    return jnp.dot(a.astype(BF16), b.astype(BF16), preferred_element_type=F32)


def _mm_nt(a, b):
    return lax.dot_general(a.astype(BF16), b.astype(BF16), (((1,), (1,)), ((), ())),
                           preferred_element_type=F32)


def _mm_tn(a, b):
    return lax.dot_general(a.astype(BF16), b.astype(BF16), (((0,), (0,)), ((), ())),
                           preferred_element_type=F32)


def _mm_f32(a, b):
    return jnp.dot(a, b, preferred_element_type=F32, precision=lax.Precision.HIGHEST)


def _mm_nt_f32(a, b):
    return lax.dot_general(a, b, (((1,), (1,)), ((), ())), preferred_element_type=F32,
                           precision=lax.Precision.HIGHEST)


def _sigmoid(x):
    return 1.0 / (1.0 + jnp.exp(-x))


def _silu(x):
    return x * _sigmoid(x)


def _ln_rows(x, g, b):
    mu = jnp.mean(x, axis=-1, keepdims=True)
    xc = x - mu
    var = jnp.mean(xc * xc, axis=-1, keepdims=True)
    return xc * lax.rsqrt(var + LN_EPS) * g + b


def _rms_scaled(o, w):
    return o * lax.rsqrt(jnp.mean(o * o, axis=-1, keepdims=True) + RMS_EPS) * w


def _shift_rows(win, sft):
    rows, cols = win.shape
    groups = rows // SUBLANES
    rot = pltpu.roll(win.reshape(groups, SUBLANES, cols), sft, axis=1)
    above = jnp.concatenate([rot[groups - 1:], rot[:groups - 1]], axis=0)
    sub = lax.broadcasted_iota(jnp.int32, rot.shape, 1)
    return jnp.where(sub < sft, above, rot).reshape(rows, cols)


def _tri(c, lower):
    r = lax.broadcasted_iota(jnp.int32, (c, c), 0)
    s = lax.broadcasted_iota(jnp.int32, (c, c), 1)
    return jnp.where(r >= s if lower else r <= s, 1.0, 0.0).astype(F32)


def _ln_kernel(x_ref, g_ref, b_ref, of_ref, ob_ref):
    y = _ln_rows(x_ref[...], g_ref[...], b_ref[...])
    of_ref[...] = y
    ob_ref[...] = y.astype(BF16)


def _layer_norm(x, g, b):
    bsz, t, d = x.shape
    n = bsz * t
    tm = min(512, n)
    of, ob = pl.pallas_call(
        _ln_kernel,
        grid=(n // tm,),
        in_specs=[pl.BlockSpec((tm, d), lambda i: (i, 0)),
                  pl.BlockSpec((1, d), lambda i: (0, 0)),
                  pl.BlockSpec((1, d), lambda i: (0, 0))],
        out_specs=[pl.BlockSpec((tm, d), lambda i: (i, 0)),
                   pl.BlockSpec((tm, d), lambda i: (i, 0))],
        out_shape=[jax.ShapeDtypeStruct((n, d), F32), jax.ShapeDtypeStruct((n, d), BF16)],
        compiler_params=_cparams("parallel"),
        name="input_ln",
    )(x.reshape(n, d), g.reshape(1, d), b.reshape(1, d))
    return of.reshape(bsz, t, d), ob.reshape(bsz, t, d)


def _in_proj_kernel(x_ref, wq_ref, wkv_ref, wb_ref, wc_ref, wg_ref, q_ref, kv_ref, b_ref, c_ref, g_ref):
    x = x_ref[...]
    dot = lambda w_ref: jnp.dot(x, w_ref[...], preferred_element_type=F32)
    q_ref[...] = (dot(wq_ref) * A_SCALE).astype(BF16)
    kv_ref[...] = dot(wkv_ref)
    hb = dot(wb_ref)
    b_ref[:, 0:B_WIDTH] = _silu(hb[:, 0:B_WIDTH])
    b_ref[:, B_WIDTH:2 * B_WIDTH] = _sigmoid(hb[:, B_WIDTH:2 * B_WIDTH])
    b_ref[:, 2 * B_WIDTH:3 * B_WIDTH] = hb[:, 2 * B_WIDTH:3 * B_WIDTH]
    b_ref[:, 3 * B_WIDTH:] = _silu(hb[:, 3 * B_WIDTH:])
    gc = dot(wc_ref)
    c_ref[:, 0:C_QKV] = gc[:, 0:C_QKV]
    c_ref[:, C_QKV:C_QKV + C_WIDTH] = _silu(gc[:, C_QKV:C_QKV + C_WIDTH])
    c_ref[:, C_QKV + C_WIDTH:] = gc[:, C_QKV + C_WIDTH:]
    g_ref[...] = _sigmoid(dot(wg_ref)).astype(BF16)


def _in_proj(x, ws, layer):
    n, k = x.shape
    tm = min(256, n)
    dts = (BF16, F32, F32, F32, BF16)
    return pl.pallas_call(
        _in_proj_kernel,
        grid=(n // tm,),
        in_specs=[pl.BlockSpec((tm, k), lambda i: (i, 0))] + [_resident(w, layer) for w in ws],
        out_specs=[pl.BlockSpec((tm, w.shape[2]), lambda i: (i, 0)) for w in ws],
        out_shape=[jax.ShapeDtypeStruct((n, w.shape[2]), dt) for w, dt in zip(ws, dts)],
        compiler_params=_cparams("parallel"),
        name="in_proj",
    )(x, *ws)


def _swa_kernel(q_ref, kvc_ref, kp_ref, vp_ref, sink_ref, o_ref, *cache_refs, prompt, qb, nsub):
    kvc = kvc_ref[...]
    kp = kp_ref[...]
    vp = vp_ref[...]
    nkeys = WINDOW + qb
    if prompt:
        row = lax.broadcasted_iota(jnp.int32, (qb, nkeys), 0)
        col = lax.broadcasted_iota(jnp.int32, (qb, nkeys), 1)
        behind = row // CHUNK - (col // CHUNK - WINDOW // CHUNK)
        band = (behind >= 0) & (behind <= WINDOW // CHUNK)
        first_valid = jnp.where(pl.program_id(1) == 0, WINDOW, 0)
        band_first = band & (col >= first_valid)
    cols = [slice(h * A_HEAD_DIM, (h + 1) * A_HEAD_DIM) for h in range(A_HEADS)]
    units = []
    for sub in range(nsub):
        rows = slice(sub * qb, (sub + 1) * qb)
        before = slice((sub - 1) * qb, sub * qb)
        for kv in range(A_KV_HEADS):
            ks = slice(kv * A_HEAD_DIM, (kv + 1) * A_HEAD_DIM)
            vs = slice(A_KV_WIDTH + kv * A_HEAD_DIM, A_KV_WIDTH + (kv + 1) * A_HEAD_DIM)
            k_prev, v_prev = (kp[:, ks], vp[:, ks]) if sub == 0 else (kvc[before, ks], kvc[before, vs])
            kk = jnp.concatenate([k_prev, kvc[rows, ks]], axis=0).astype(BF16)
            vv = jnp.concatenate([v_prev, kvc[rows, vs]], axis=0).astype(BF16)
            units += [(sub, kv * A_GROUP + g, rows, kk, vv) for g in range(A_GROUP)]
    scores = [_mm_nt(q_ref[rows, cols[h]], kk) for _, h, rows, kk, _ in units]
    es, dens = [], []
    for (sub, h, _, _, _), s in zip(units, scores):
        if prompt:
            s = jnp.where(band_first if sub == 0 else band, s, NEG_BIG)
        sink = sink_ref[h]
        m = jnp.maximum(jnp.max(s, axis=-1, keepdims=True), sink)
        e = jnp.exp(s - m)
        es.append(e.astype(BF16))
        dens.append(jnp.sum(e, axis=-1, keepdims=True) + jnp.exp(sink - m))
    outs = [_mm(e, vv) for e, (_, _, _, _, vv) in zip(es, units)]
    for (_, h, rows, _, _), o, den in zip(units, outs, dens):
        o_ref[rows, cols[h]] = (o / den).astype(o_ref.dtype)
    if not prompt:
        nk_ref, nv_ref = cache_refs
        keep = WINDOW - qb
        nk_ref[0:keep, :] = kp[qb:, :]
        nk_ref[keep:, :] = kvc[:, 0:A_KV_WIDTH]
        nv_ref[0:keep, :] = vp[qb:, :]
        nv_ref[keep:, :] = kvc[:, A_KV_WIDTH:]


def _swa(sq, skv, k_cache, v_cache, sink, prompt):
    bsz, t, _ = sq.shape
    if prompt:
        qb = WINDOW
        nsub = 2 if t % (2 * qb) == 0 else 1
        assert t % (nsub * qb) == 0
        kp_spec = pl.BlockSpec((None, WINDOW, A_KV_WIDTH), lambda b, i: (b, jnp.maximum(i * nsub - 1, 0), 0))
        vp_spec = pl.BlockSpec((None, WINDOW, A_KV_WIDTH), lambda b, i: (b, jnp.maximum(i * nsub - 1, 0), 1))
        kp_arr, vp_arr = skv, skv
        out_shape = [jax.ShapeDtypeStruct((bsz, t, A_WIDTH), BF16)]
        out_specs = [pl.BlockSpec((None, nsub * qb, A_WIDTH), lambda b, i: (b, i, 0))]
    else:
        qb = t
        nsub = 1
        assert t <= WINDOW and t % (2 * SUBLANES) == 0 and k_cache.shape[1] == WINDOW
        kp_spec = pl.BlockSpec((None, WINDOW, A_KV_WIDTH), lambda b, i: (b, 0, 0))
        vp_spec = kp_spec
        kp_arr = k_cache.reshape(bsz, WINDOW, A_KV_WIDTH)
        vp_arr = v_cache.reshape(bsz, WINDOW, A_KV_WIDTH)
        out_shape = [jax.ShapeDtypeStruct((bsz, t, A_WIDTH), BF16),
                     jax.ShapeDtypeStruct((bsz, WINDOW, A_KV_WIDTH), F32),
                     jax.ShapeDtypeStruct((bsz, WINDOW, A_KV_WIDTH), F32)]
        cache_spec = pl.BlockSpec((None, WINDOW, A_KV_WIDTH), lambda b, i: (b, 0, 0))
        out_specs = [pl.BlockSpec((None, qb, A_WIDTH), lambda b, i: (b, i, 0)), cache_spec, cache_spec]
    outs = pl.pallas_call(
        functools.partial(_swa_kernel, prompt=prompt, qb=qb, nsub=nsub),
        grid=(bsz, t // (nsub * qb)),
        in_specs=[pl.BlockSpec((None, nsub * qb, A_WIDTH), lambda b, i: (b, i, 0)),
                  pl.BlockSpec((None, nsub * qb, 2 * A_KV_WIDTH), lambda b, i: (b, i, 0)),
                  kp_spec, vp_spec,
                  pl.BlockSpec(memory_space=pltpu.SMEM)],
        out_specs=out_specs,
        out_shape=out_shape,
        compiler_params=_cparams("parallel", "parallel"),
        name="swa",
    )(sq, skv, kp_arr, vp_arr, sink)
    if prompt:
        new_k = skv[:, t - WINDOW:, 0:A_KV_WIDTH]
        new_v = skv[:, t - WINDOW:, A_KV_WIDTH:]
    else:
        new_k, new_v = outs[1], outs[2]
    shape = (bsz, WINDOW, A_KV_HEADS, A_HEAD_DIM)
    return outs[0], new_k.reshape(shape), new_v.reshape(shape)


def _hgrn_kernel(x_ref, lb_ref, w_ref, s0_ref, o_ref, s_ref, st_ref, *, c):
    tb, tt, _ = x_ref.shape
    ti = pl.program_id(1)

    @pl.when(ti == 0)
    def _():
        for b in range(tb):
            for h in range(B_HEADS):
                st_ref[b, h] = s0_ref[b, h].T

    pos = lax.broadcasted_iota(jnp.int32, (tt, B_DIM), 0)
    r = lax.broadcasted_iota(jnp.int32, (c, c), 0)
    s = lax.broadcasted_iota(jnp.int32, (c, c), 1)
    x = r ^ s
    sizes = [1 << i for i in range(int(math.log2(c)))]
    levels = {hh: (r > s) & (x >= hh) & (x < 2 * hh) for hh in sizes}
    uppers = {hh: (pos % (2 * hh)) >= hh for hh in sizes}
    heads = []
    for b in range(tb):
        for h in range(B_HEADS):
            cs = slice(h * B_DIM, (h + 1) * B_DIM)
            q = x_ref[b, :, cs]
            sig = x_ref[b, :, B_WIDTH + h * B_DIM:B_WIDTH + (h + 1) * B_DIM]
            v = x_ref[b, :, 2 * B_WIDTH + h * B_DIM:2 * B_WIDTH + (h + 1) * B_DIM]
            gate = x_ref[b, :, 3 * B_WIDTH + h * B_DIM:3 * B_WIDTH + (h + 1) * B_DIM]
            lb = lb_ref[:, cs]
            f = jnp.maximum(lb + (1.0 - lb) * sig, F_MIN)
            k = (1.0 - lb) * (1.0 - sig)
            total = f
            qd = q * f
            kd = k
            att = [jnp.where(r == s, jnp.sum((q * k)[j * c:(j + 1) * c], axis=-1, keepdims=True), 0.0)
                   for j in range(tt // c)]
            hh = 1
            while hh < c:
                qh = qd.astype(BF16)
                kh = kd.astype(BF16)
                for j in range(tt // c):
                    rows = slice(j * c, (j + 1) * c)
                    att[j] = jnp.where(levels[hh], _mm_nt(qh[rows], kh[rows]), att[j])
                below = pltpu.roll(total, hh, axis=0)
                above = pltpu.roll(total, tt - hh, axis=0)
                qd = jnp.where(uppers[hh], qd * below, qd)
                kd = jnp.where(uppers[hh], kd, kd * above)
                total = total * jnp.where(uppers[hh], below, above)
                hh *= 2
            heads.append(dict(b=b, h=h, cs=cs, att=att, qc=qd.astype(BF16), kc=kd.astype(BF16), v=v.astype(BF16),
                              gate=gate, total=total))
    chunks = [slice(j * c, (j + 1) * c) for j in range(tt // c)]
    for hd in heads:
        hd["av"] = [_mm(hd["att"][j], hd["v"][rows]) for j, rows in enumerate(chunks)]
        hd["kv"] = [_mm_tn(hd["v"][rows], hd["kc"][rows]) for rows in chunks]
    for j, rows in enumerate(chunks):
        for hd in heads:
            b, h = hd["b"], hd["h"]
            st = st_ref[b, h]
            o = _mm_nt(hd["qc"][rows], st) + hd["av"][j]
            st_ref[b, h] = st * hd["total"][(j + 1) * c - 1:(j + 1) * c, :] + hd["kv"][j]
            o_ref[b, rows, hd["cs"]] = (_rms_scaled(o, w_ref[...]) * hd["gate"][rows]).astype(o_ref.dtype)

    @pl.when(ti == pl.num_programs(1) - 1)
    def _():
        for b in range(tb):
            for h in range(B_HEADS):
                s_ref[b, h] = st_ref[b, h].T


def _hgrn(sb, lb, norm_w, state, tb, tt):
    bsz, t, _ = sb.shape
    c = min(CHUNK, t)
    assert tt % c == 0 and t % tt == 0 and bsz % tb == 0
    st_spec = pl.BlockSpec((tb, B_HEADS, B_DIM, B_DIM), lambda b, i: (b, 0, 0, 0))
    return pl.pallas_call(
        functools.partial(_hgrn_kernel, c=c),
        grid=(bsz // tb, t // tt),
        in_specs=[pl.BlockSpec((tb, tt, 4 * B_WIDTH), lambda b, i: (b, i, 0)),
                  pl.BlockSpec((1, B_WIDTH), lambda b, i: (0, 0)),
                  pl.BlockSpec((1, B_DIM), lambda b, i: (0, 0)),
                  st_spec],
        out_specs=[pl.BlockSpec((tb, tt, B_WIDTH), lambda b, i: (b, i, 0)), st_spec],
        out_shape=[jax.ShapeDtypeStruct((bsz, t, B_WIDTH), BF16),
                   jax.ShapeDtypeStruct((bsz, B_HEADS, B_DIM, B_DIM), F32)],
        scratch_shapes=[pltpu.VMEM((tb, B_HEADS, B_DIM, B_DIM), F32)],
        compiler_params=_cparams("parallel", "arbitrary"),
        name="hgrn",
    )(sb, lb.reshape(1, B_WIDTH), norm_w.reshape(1, B_DIM), state)


def _gdn_prep_kernel(x_ref, sm_ref, prev_ref, b0_ref, cw_ref, alog_ref, dtb_ref,
                     sv_ref, sk_ref, qk_ref, qt_ref, kt_ref, ec_ref, cbuf_ref, *, c):
    tb, tt, _ = x_ref.shape
    pad = SUBLANES
    keep = C_CONV - 1
    ti = pl.program_id(1)

    @pl.when(ti == 0)
    def _():
        cbuf_ref[:, pad - keep:pad, :] = b0_ref[...]

    @pl.when(ti > 0)
    def _():
        cbuf_ref[:, pad - keep:pad, :] = prev_ref[:, pad - keep:pad, :]

    cbuf_ref[:, 0:pad - keep, :] = jnp.zeros((tb, pad - keep, C_QKV), F32)
    cbuf_ref[:, pad:pad + tt, :] = x_ref[...]
    win = cbuf_ref[...].reshape(tb * (pad + tt), C_QKV)
    u = win * cw_ref[keep:keep + 1, :]
    for sft in range(1, C_CONV):
        u = u + _shift_rows(win, sft) * cw_ref[keep - sft:keep - sft + 1, :]
    u = _silu(u.reshape(tb, pad + tt, C_QKV)[:, pad:, :])

    small = sm_ref[...]
    pre = small + dtb_ref[...]
    softplus = jnp.maximum(pre, 0.0) + jnp.log(1.0 + jnp.exp(-jnp.abs(pre)))
    g_all = -jnp.exp(alog_ref[...]) * softplus
    beta_all = _sigmoid(small)
    tr = lax.broadcasted_iota(jnp.int32, (tt, tt), 0)
    ts = lax.broadcasted_iota(jnp.int32, (tt, tt), 1)
    chunk_tri = jnp.where((tr >= ts) & (tr // c == ts // c), 1.0, 0.0).astype(F32)
    sel_r = lax.broadcasted_iota(jnp.int32, (SUBLANES, LANES), 0)
    sel_c = lax.broadcasted_iota(jnp.int32, (SUBLANES, LANES), 1)
    sel = jnp.where(sel_r == sel_c, 1.0, 0.0).astype(F32)
    r = lax.broadcasted_iota(jnp.int32, (c, c), 0)
    s = lax.broadcasted_iota(jnp.int32, (c, c), 1)
    incl = r >= s
    units = []
    for b in range(tb):
        cum_all = _mm_f32(chunk_tri, g_all[b])
        cum_rows = _mm_nt_f32(sel, cum_all)
        ecum_all = jnp.exp(cum_all)
        ec_ref[b] = ecum_all
        for h in range(C_HEADS):
            cs = slice(h * C_DIM, (h + 1) * C_DIM)
            uq = u[b, :, cs]
            uk = u[b, :, C_WIDTH + h * C_DIM:C_WIDTH + (h + 1) * C_DIM]
            v = u[b, :, 2 * C_WIDTH + h * C_DIM:2 * C_WIDTH + (h + 1) * C_DIM]
            q = uq * lax.rsqrt(jnp.sum(uq * uq, axis=-1, keepdims=True) + 1e-6) * (C_DIM ** -0.5)
            k = uk * lax.rsqrt(jnp.sum(uk * uk, axis=-1, keepdims=True) + 1e-6)
            beta = beta_all[b, :, h:h + 1]
            cum = cum_all[:, C_HEADS + h:C_HEADS + h + 1]
            ecum = ecum_all[:, C_HEADS + h:C_HEADS + h + 1]
            qt_ref[b, :, cs] = (q * ecum).astype(BF16)
            rhs = jnp.concatenate([v * beta, k * (beta * ecum)], axis=-1)
            for j in range(tt // c):
                rows = slice(j * c, (j + 1) * c)
                cum_j = cum[rows]
                cum_row = cum_rows[C_HEADS + h:C_HEADS + h + 1, rows]
                dec = jnp.where(incl, jnp.exp(jnp.where(incl, cum_j - cum_row, 0.0)), 0.0)
                k_j = k[rows]
                kb = k_j.astype(BF16)
                last = cum_j[c - 1:c, :]
                kt_ref[b, rows, cs] = (k_j * jnp.exp(last - cum_j)).astype(BF16)
                units.append(dict(b=b, rows=rows, cs=cs, qcols=slice(h * c, (h + 1) * c), dec=dec, kb=kb,
                                  qb=q[rows].astype(BF16), beta=beta[rows], rhs=rhs[rows]))
    for un in units:
        un["kk"] = _mm_nt(un["kb"], un["kb"])
        un["qk"] = _mm_nt(un["qb"], un["kb"])
    for un in units:
        qk_ref[un["b"], un["rows"], un["qcols"]] = (un["qk"] * un["dec"]).astype(BF16)
        n = jnp.where(r > s, un["beta"] * un["kk"] * un["dec"], 0.0)
        un["p"] = n
        un["corr"] = -n
    for _ in range(int(math.log2(c)) - 1):
        for un in units:
            pb = un["p"].astype(BF16)
            un["p"] = jnp.dot(pb, pb, preferred_element_type=F32)
        for un in units:
            un["corr"] = un["corr"] + un["p"] + _mm(un["corr"], un["p"])
    for un in units:
        un["sol"] = un["rhs"] + _mm(un["corr"], un["rhs"])
    for un in units:
        sv_ref[un["b"], un["rows"], un["cs"]] = un["sol"][:, 0:C_DIM]
        sk_ref[un["b"], un["rows"], un["cs"]] = un["sol"][:, C_DIM:].astype(BF16)


def _gdn_scan_kernel(sv_ref, sk_ref, qk_ref, qt_ref, kt_ref, ec_ref, z_ref, w_ref, s0_ref,
                     o_ref, s_ref, st_ref, *, c):
    tb, tt, _ = sv_ref.shape
    ti = pl.program_id(1)

    @pl.when(ti == 0)
    def _():
        st_ref[...] = s0_ref[...]

    heads = [(b, h, slice(h * C_DIM, (h + 1) * C_DIM)) for b in range(tb) for h in range(C_HEADS)]
    for j in range(tt // c):
        rows = slice(j * c, (j + 1) * c)
        st = [st_ref[b, h] for b, h, _ in heads]
        stb = [a.astype(BF16) for a in st]
        sks = [_mm(sk_ref[b, rows, cs], stb[i]) for i, (b, h, cs) in enumerate(heads)]
        qs = [_mm(qt_ref[b, rows, cs], stb[i]) for i, (b, h, cs) in enumerate(heads)]
        wb = [(sv_ref[b, rows, cs] - sks[i]).astype(BF16) for i, (b, h, cs) in enumerate(heads)]
        upd = [_mm_tn(kt_ref[b, rows, cs], wb[i]) for i, (b, h, cs) in enumerate(heads)]
        qkw = [_mm(qk_ref[b, rows, h * c:(h + 1) * c], wb[i]) for i, (b, h, cs) in enumerate(heads)]
        for i, (b, h, cs) in enumerate(heads):
            e_last = ec_ref[b, (j + 1) * c - 1:(j + 1) * c, C_HEADS + h:C_HEADS + h + 1]
            st_ref[b, h] = e_last * st[i] + upd[i]
            o_ref[b, rows, cs] = (_rms_scaled(qs[i] + qkw[i], w_ref[...]) * z_ref[b, rows, cs]).astype(o_ref.dtype)

    @pl.when(ti == pl.num_programs(1) - 1)
    def _():
        s_ref[...] = st_ref[...]


def _gdn(sc, conv_w, a_log, dt_bias, norm_w, state, buf, tb_prep, tb_scan, tt):
    bsz, t, _ = sc.shape
    c = min(CHUNK, t)
    keep = C_CONV - 1
    assert t >= keep and tt % c == 0 and t % tt == 0
    pad_vec = lambda a: jnp.zeros((1, LANES), F32).at[0, C_HEADS:2 * C_HEADS].set(a.astype(F32))
    const = lambda shape: pl.BlockSpec(shape, lambda b, i: (0,) * len(shape))
    tile = lambda tb, w, col=0: pl.BlockSpec((tb, tt, w), lambda b, i: (b, i, col))
    shape = lambda w, dt: jax.ShapeDtypeStruct((bsz, t, w), dt)
    small_col = (C_QKV + C_WIDTH) // LANES
    prev_spec = pl.BlockSpec((tb_prep, SUBLANES, C_QKV),
                             lambda b, i: (b, jnp.maximum(i * (tt // SUBLANES) - 1, 0), 0))
    sv, sk, qk, qt, kt, ec = pl.pallas_call(
        functools.partial(_gdn_prep_kernel, c=c),
        grid=(bsz // tb_prep, t // tt),
        in_specs=[tile(tb_prep, C_QKV), tile(tb_prep, LANES, small_col), prev_spec,
                  pl.BlockSpec((tb_prep, keep, C_QKV), lambda b, i: (b, 0, 0)),
                  const((C_CONV, C_QKV)), const((1, LANES)), const((1, LANES))],
        out_specs=[tile(tb_prep, C_WIDTH), tile(tb_prep, C_WIDTH), tile(tb_prep, C_HEADS * c),
                   tile(tb_prep, C_WIDTH), tile(tb_prep, C_WIDTH), tile(tb_prep, LANES)],
        out_shape=[shape(C_WIDTH, F32), shape(C_WIDTH, BF16), shape(C_HEADS * c, BF16),
                   shape(C_WIDTH, BF16), shape(C_WIDTH, BF16), shape(LANES, F32)],
        scratch_shapes=[pltpu.VMEM((tb_prep, SUBLANES + tt, C_QKV), F32)],
        compiler_params=_cparams("parallel", "parallel"),
        name="gdn_prep",
    )(sc, sc, sc, buf, conv_w, pad_vec(a_log), pad_vec(dt_bias))
    st_spec = pl.BlockSpec((tb_scan, C_HEADS, C_DIM, C_DIM), lambda b, i: (b, 0, 0, 0))
    o, new_state = pl.pallas_call(
        functools.partial(_gdn_scan_kernel, c=c),
        grid=(bsz // tb_scan, t // tt),
        in_specs=[tile(tb_scan, C_WIDTH), tile(tb_scan, C_WIDTH), tile(tb_scan, C_HEADS * c),
                  tile(tb_scan, C_WIDTH), tile(tb_scan, C_WIDTH), tile(tb_scan, LANES),
                  tile(tb_scan, C_WIDTH, C_QKV // C_WIDTH), const((1, C_DIM)), st_spec],
        out_specs=[tile(tb_scan, C_WIDTH), st_spec],
        out_shape=[shape(C_WIDTH, BF16), jax.ShapeDtypeStruct((bsz, C_HEADS, C_DIM, C_DIM), F32)],
        scratch_shapes=[pltpu.VMEM((tb_scan, C_HEADS, C_DIM, C_DIM), F32)],
        compiler_params=_cparams("parallel", "arbitrary"),
        name="gdn_scan",
    )(sv, sk, qk, qt, kt, ec, sc, norm_w.reshape(1, C_DIM), state)
    return o, new_state, sc[:, t - keep:, 0:C_QKV]


def _merge_rows(oa, ob, oc, gates, x, wa, wb, wc, wo, ln_g, ln_b, alpha):
    d = x.shape[-1]
    mix = gates[:, 0:d].astype(F32) * _mm(oa, wa)
    mix = mix + gates[:, d:2 * d].astype(F32) * _mm(ob, wb)
    mix = mix + gates[:, 2 * d:3 * d].astype(F32) * _mm(oc, wc)
    return _ln_rows(alpha * x + _mm(mix, wo), ln_g, ln_b)


def _ffn_kernel(oa_ref, ob_ref, oc_ref, g_ref, x_ref, wa_ref, wb_ref, wc_ref, wo_ref, l1g_ref, l1b_ref,
                b0_ref, wu_ref, cw_ref, cb_ref, wd_ref, lg_ref, lb_ref,
                yf_ref, yb_ref, b_ref, carry_ref, act_ref, *, alpha):
    tb, tt, d = x_ref.shape
    d_ff = wd_ref.shape[0]
    m = tb * tt
    nj = d_ff // FFN_COLS
    keep = F_CONV - 1
    pad = SUBLANES
    ti = pl.program_id(1)

    @pl.when(ti == 0)
    def _():
        carry_ref[:, 0:pad - keep, :] = jnp.zeros((tb, pad - keep, carry_ref.shape[-1]), F32)
        carry_ref[:, pad - keep:, :] = b0_ref[...]

    flat = lambda ref: ref[...].reshape(m, ref.shape[-1])
    x = _merge_rows(flat(oa_ref), flat(ob_ref), flat(oc_ref), flat(g_ref), flat(x_ref),
                    wa_ref[...], wb_ref[...], wc_ref[...], wo_ref[...], l1g_ref[...], l1b_ref[...], alpha)
    xb = x.astype(BF16)

    def up(c0):
        return jnp.dot(xb, wu_ref[:, c0:c0 + FFN_COLS], preferred_element_type=F32)

    def conv(c0, rows):
        cols = slice(c0, c0 + FFN_COLS)
        rows = rows.reshape(tb, tt, FFN_COLS)
        win = jnp.concatenate([carry_ref[:, :, cols], rows], axis=1).reshape(tb * (pad + tt), FFN_COLS)
        out = win * cw_ref[keep:keep + 1, cols]
        for sft in range(1, F_CONV):
            out = out + _shift_rows(win, sft) * cw_ref[keep - sft:keep - sft + 1, cols]
        carry_ref[:, pad - keep:, cols] = rows[:, tt - keep:, :]
        return out.reshape(tb, pad + tt, FFN_COLS)[:, pad:, :] + cb_ref[:, cols]

    parts = [0, (nj + 2) // 3, (2 * nj + 2) // 3, nj]
    down = None
    nxt = (up(0), up(d_ff))
    for j in range(nj):
        cur = nxt
        if j + 1 < nj:
            nxt = (up((j + 1) * FFN_COLS), up(d_ff + (j + 1) * FFN_COLS))
        gate = conv(j * FFN_COLS, cur[0])
        val = conv(d_ff + j * FFN_COLS, cur[1])
        act_ref[:, j * FFN_COLS:(j + 1) * FFN_COLS] = (_silu(gate) * val).reshape(m, FFN_COLS).astype(BF16)
        if j + 1 in parts:
            lo = parts[parts.index(j + 1) - 1] * FFN_COLS
            part = jnp.dot(act_ref[:, lo:(j + 1) * FFN_COLS], wd_ref[lo:(j + 1) * FFN_COLS, :],
                           preferred_element_type=F32)
            down = part if down is None else down + part
    y = _ln_rows(alpha * x + down, lg_ref[...], lb_ref[...]).reshape(tb, tt, d)
    yf_ref[...] = y
    yb_ref[...] = y.astype(BF16)

    @pl.when(ti == pl.num_programs(1) - 1)
    def _():
        b_ref[...] = carry_ref[:, pad - keep:, :]


def _merge_ffn(oa, ob, oc, gates, x, w_branch, wout, ln1_g, ln1_b,
               buf, wu, cw, cb, wd, layer, ln_g, ln_b, alpha, tb, tt):
    bsz, t, d = x.shape
    d_ff = wd.shape[1]
    keep = F_CONV - 1
    assert A_WIDTH == B_WIDTH == C_WIDTH
    tile = lambda w: pl.BlockSpec((tb, tt, w), lambda b, i: (b, i, 0))
    row = tile(d)
    branch = lambda k: _resident(w_branch, layer, row_block=k, rows=A_WIDTH)
    buf_spec = pl.BlockSpec((tb, keep, 2 * d_ff), lambda b, i: (b, 0, 0))
    const = lambda a: pl.BlockSpec(a.shape, lambda b, i: (0,) * a.ndim, pipeline_mode=pl.Buffered(1))
    l1g = ln1_g.reshape(1, d)
    l1b = ln1_b.reshape(1, d)
    lg = ln_g.reshape(1, d)
    lb = ln_b.reshape(1, d)
    cb = cb.reshape(1, 2 * d_ff)
    return pl.pallas_call(
        functools.partial(_ffn_kernel, alpha=alpha),
        grid=(bsz // tb, t // tt),
        in_specs=[tile(A_WIDTH), tile(B_WIDTH), tile(C_WIDTH), tile(N_BRANCH * d), row,
                  branch(0), branch(1), branch(2), _resident(wout, layer), const(l1g), const(l1b),
                  buf_spec, _resident(wu, layer), const(cw), const(cb), _resident(wd, layer),
                  const(lg), const(lb)],
        out_specs=[row, row, buf_spec],
        out_shape=[jax.ShapeDtypeStruct((bsz, t, d), F32),
                   jax.ShapeDtypeStruct((bsz, t, d), BF16),
                   jax.ShapeDtypeStruct((bsz, keep, 2 * d_ff), F32)],
        scratch_shapes=[pltpu.VMEM((tb, SUBLANES, 2 * d_ff), F32),
                        pltpu.VMEM((tb * tt, d_ff), BF16)],
        compiler_params=_cparams("parallel", "arbitrary"),
        name="merge_ffn",
    )(oa, ob, oc, gates, x, w_branch, w_branch, w_branch, wout, l1g, l1b, buf, wu, cw, cb, wd, lg, lb)


def _layer(xf, xb, state, p, prompt, alpha):
    k_cache, v_cache, s_hgrn, s_gdn, buf_gdn, buf_ffn = state
    bsz, t, d = xf.shape
    n = bsz * t
    x2 = xb.reshape(n, d)
    sq, skv, sb, sc, sg = _in_proj(x2, p["w_in"], p["layer"])
    sq, skv, sb, sc = (a.reshape(bsz, t, -1) for a in (sq, skv, sb, sc))
    oa, new_k, new_v = _swa(sq, skv, k_cache, v_cache, p["sink"], prompt)
    ob, new_hgrn = _hgrn(sb, p["lb"], p["hgrn_w"], s_hgrn, *((1, 256) if prompt else (8, t)))
    gdn_tiles = (1, bsz, 256) if prompt else (8, 8, t)
    oc, new_gdn, new_gconv = _gdn(sc, p["gdn_conv_w"], p["a_log"], p["dt_bias"], p["gdn_w"], s_gdn, buf_gdn,
                                  *gdn_tiles)
    tb, tt = (1, 256) if prompt else (8, t)
    zf, zb, new_fconv = _merge_ffn(oa, ob, oc, sg.reshape(bsz, t, -1), xf, p["w_branch"], p["wout"],
                                   p["ln1_g"], p["ln1_b"], buf_ffn, p["wu"], p["fcw"], p["fcb"], p["wd"],
                                   p["layer"], p["ln2_g"], p["ln2_b"], alpha, tb, tt)
    return zf, zb, (new_k, new_v, new_hgrn, new_gdn, new_gconv, new_fconv)


def kernel(x_prompt, x_sample, cache_swa_k, cache_swa_v, state_hgrn, state_gdn, state_gdn_conv, state_ffn_conv, ln_in_g, ln_in_b, w_in, attn_sinks, hgrn_lb_logits, hgrn_norm_w, gdn_conv_w, gdn_a_log, gdn_dt_bias, gdn_norm_w, w_branch, w_out, ln1_g, ln1_b, w_up, ffn_conv_w, ffn_conv_b, w_down, ln2_g, ln2_b):
    depth, d, _ = w_in.shape
    alpha = (2 * depth) ** 0.25
    lb_sm = jax.nn.softmax(hgrn_lb_logits.astype(F32), axis=0)
    lower_bounds = jnp.cumsum(lb_sm, axis=0) - lb_sm[0]

    o_a = A_WIDTH + 2 * A_KV_WIDTH
    o_b = o_a + 4 * B_WIDTH
    o_c = o_b + C_QKV + C_WIDTH
    o_s = o_c + 2 * C_HEADS
    wq = w_in[:, :, 0:A_WIDTH].astype(BF16)
    wkv = w_in[:, :, A_WIDTH:o_a].astype(BF16)
    wb = w_in[:, :, o_a:o_b].astype(BF16)
    wc = jnp.concatenate([w_in[:, :, o_b:o_s], jnp.zeros((depth, d, LANES - 2 * C_HEADS), w_in.dtype)],
                         axis=-1).astype(BF16)
    wg = w_in[:, :, o_s:].astype(BF16)
    w_in_groups = (wq, wkv, wb, wc, wg)
    w_branch_b = w_branch.astype(BF16)
    wout = w_out.astype(BF16)
    wu, fcw, fcb, wd = w_up.astype(BF16), ffn_conv_w.astype(F32), ffn_conv_b.astype(F32), w_down.astype(BF16)

    hp_f, hp_b = _layer_norm(x_prompt, ln_in_g, ln_in_b)
    hs_f, hs_b = _layer_norm(x_sample, ln_in_g, ln_in_b)
    bp = x_prompt.shape[0]
    prompt_state = (None, None,
                    jnp.zeros((bp, B_HEADS, B_DIM, B_DIM), F32),
                    jnp.zeros((bp, C_HEADS, C_DIM, C_DIM), F32),
                    jnp.zeros((bp, C_CONV - 1, C_QKV), F32),
                    jnp.zeros((bp, F_CONV - 1, w_up.shape[-1]), F32))
    p_new, s_new = [], []
    for l in range(depth):
        p = dict(layer=l, w_in=w_in_groups, sink=attn_sinks[l].astype(F32), lb=lower_bounds[l],
                 hgrn_w=hgrn_norm_w[l], gdn_conv_w=gdn_conv_w[l], a_log=gdn_a_log[l], dt_bias=gdn_dt_bias[l],
                 gdn_w=gdn_norm_w[l], w_branch=w_branch_b, wout=wout,
                 ln1_g=ln1_g[l], ln1_b=ln1_b[l], wu=wu, fcw=fcw[l], fcb=fcb[l], wd=wd,
                 ln2_g=ln2_g[l], ln2_b=ln2_b[l])
        hp_f, hp_b, st_p = _layer(hp_f, hp_b, prompt_state, p, True, alpha)
        sample_state = (cache_swa_k[l], cache_swa_v[l], state_hgrn[l], state_gdn[l],
                        state_gdn_conv[l], state_ffn_conv[l])
        hs_f, hs_b, st_s = _layer(hs_f, hs_b, sample_state, p, False, alpha)
        p_new.append(st_p)
        s_new.append(st_s)

    def stack(states, i):
        return jnp.stack([st[i] for st in states])

    return (hp_f, hs_f) + tuple(stack(p_new, i) for i in range(6)) + tuple(stack(s_new, i) for i in range(6))
```

```python
import functools
import math

import jax
import jax.numpy as jnp
from jax import lax
from jax.experimental import pallas as pl
from jax.experimental.pallas import tpu as pltpu

F32 = jnp.float32
BF16 = jnp.bfloat16

CHUNK = 64
WINDOW = 128
A_HEADS = 8
A_KV_HEADS = 2
A_GROUP = A_HEADS // A_KV_HEADS
A_HEAD_DIM = 64
A_SCALE = A_HEAD_DIM ** -0.5
A_WIDTH = A_HEADS * A_HEAD_DIM
A_KV_WIDTH = A_KV_HEADS * A_HEAD_DIM
B_HEADS = 4
B_DIM = 128
C_HEADS = 4
C_DIM = 128
C_CONV = 4
F_CONV = 3
N_BRANCH = 3
B_WIDTH = B_HEADS * B_DIM
C_WIDTH = C_HEADS * C_DIM
C_QKV = 3 * C_WIDTH
LN_EPS = 1e-5
RMS_EPS = 1e-6
NEG_BIG = -1e30
F_MIN = 1e-30

LANES = 128
SUBLANES = 8
FFN_COLS = 256
VMEM_LIMIT = 48 * 1024 * 1024


def _cparams(*sem):
    return pltpu.CompilerParams(dimension_semantics=sem, vmem_limit_bytes=VMEM_LIMIT)


def _resident(a, layer, row_block=None, rows=None):
    shape = (None, a.shape[1] if rows is None else rows, a.shape[2])
    rb = 0 if row_block is None else row_block
    return pl.BlockSpec(shape, lambda *_: (layer, rb, 0), pipeline_mode=pl.Buffered(1))


def _mm(a, b):
    return jnp.dot(a.astype(BF16), b.astype(BF16), preferred_element_type=F32)


def _mm_nt(a, b):
    return lax.dot_general(a.astype(BF16), b.astype(BF16), (((1,), (1,)), ((), ())),
                           preferred_element_type=F32)


def _mm_tn(a, b):
    return lax.dot_general(a.astype(BF16), b.astype(BF16), (((0,), (0,)), ((), ())),
                           preferred_element_type=F32)


def _mm_f32(a, b):
    return jnp.dot(a, b, preferred_element_type=F32, precision=lax.Precision.HIGHEST)


def _mm_nt_f32(a, b):
    return lax.dot_general(a, b, (((1,), (1,)), ((), ())), preferred_element_type=F32,
                           precision=lax.Precision.HIGHEST)


def _sigmoid(x):
    return 1.0 / (1.0 + jnp.exp(-x))


def _silu(x):
    return x * _sigmoid(x)


def _ln_rows(x, g, b):
    mu = jnp.mean(x, axis=-1, keepdims=True)
    xc = x - mu
    var = jnp.mean(xc * xc, axis=-1, keepdims=True)
    return xc * lax.rsqrt(var + LN_EPS) * g + b


def _rms_scaled(o, w):
    return o * lax.rsqrt(jnp.mean(o * o, axis=-1, keepdims=True) + RMS_EPS) * w


def _shift_rows(win, sft):
    rows, cols = win.shape
    groups = rows // SUBLANES
    rot = pltpu.roll(win.reshape(groups, SUBLANES, cols), sft, axis=1)
    above = jnp.concatenate([rot[groups - 1:], rot[:groups - 1]], axis=0)
    sub = lax.broadcasted_iota(jnp.int32, rot.shape, 1)
    return jnp.where(sub < sft, above, rot).reshape(rows, cols)


def _tri(c, lower):
    r = lax.broadcasted_iota(jnp.int32, (c, c), 0)
    s = lax.broadcasted_iota(jnp.int32, (c, c), 1)
    return jnp.where(r >= s if lower else r <= s, 1.0, 0.0).astype(F32)


def _ln_kernel(x_ref, g_ref, b_ref, of_ref, ob_ref):
    y = _ln_rows(x_ref[...], g_ref[...], b_ref[...])
    of_ref[...] = y
    ob_ref[...] = y.astype(BF16)


def _layer_norm(x, g, b):
    bsz, t, d = x.shape
    n = bsz * t
    tm = min(512, n)
    of, ob = pl.pallas_call(
        _ln_kernel,
        grid=(n // tm,),
        in_specs=[pl.BlockSpec((tm, d), lambda i: (i, 0)),
                  pl.BlockSpec((1, d), lambda i: (0, 0)),
                  pl.BlockSpec((1, d), lambda i: (0, 0))],
        out_specs=[pl.BlockSpec((tm, d), lambda i: (i, 0)),
                   pl.BlockSpec((tm, d), lambda i: (i, 0))],
        out_shape=[jax.ShapeDtypeStruct((n, d), F32), jax.ShapeDtypeStruct((n, d), BF16)],
        compiler_params=_cparams("parallel"),
        name="input_ln",
    )(x.reshape(n, d), g.reshape(1, d), b.reshape(1, d))
    return of.reshape(bsz, t, d), ob.reshape(bsz, t, d)


def _in_proj_kernel(x_ref, wq_ref, wkv_ref, wb_ref, wc_ref, wg_ref, q_ref, kv_ref, b_ref, c_ref, g_ref):
    x = x_ref[...]
    dot = lambda w_ref: jnp.dot(x, w_ref[...], preferred_element_type=F32)
    q_ref[...] = (dot(wq_ref) * A_SCALE).astype(BF16)
    kv_ref[...] = dot(wkv_ref)
    hb = dot(wb_ref)
    b_ref[:, 0:B_WIDTH] = _silu(hb[:, 0:B_WIDTH])
    b_ref[:, B_WIDTH:2 * B_WIDTH] = _sigmoid(hb[:, B_WIDTH:2 * B_WIDTH])
    b_ref[:, 2 * B_WIDTH:3 * B_WIDTH] = hb[:, 2 * B_WIDTH:3 * B_WIDTH]
    b_ref[:, 3 * B_WIDTH:] = _silu(hb[:, 3 * B_WIDTH:])
    gc = dot(wc_ref)
    c_ref[:, 0:C_QKV] = gc[:, 0:C_QKV]
    c_ref[:, C_QKV:C_QKV + C_WIDTH] = _silu(gc[:, C_QKV:C_QKV + C_WIDTH])
    c_ref[:, C_QKV + C_WIDTH:] = gc[:, C_QKV + C_WIDTH:]
    g_ref[...] = _sigmoid(dot(wg_ref)).astype(BF16)


def _in_proj(x, ws, layer):
    n, k = x.shape
    tm = min(256, n)
    dts = (BF16, F32, F32, F32, BF16)
    return pl.pallas_call(
        _in_proj_kernel,
        grid=(n // tm,),
        in_specs=[pl.BlockSpec((tm, k), lambda i: (i, 0))] + [_resident(w, layer) for w in ws],
        out_specs=[pl.BlockSpec((tm, w.shape[2]), lambda i: (i, 0)) for w in ws],
        out_shape=[jax.ShapeDtypeStruct((n, w.shape[2]), dt) for w, dt in zip(ws, dts)],
        compiler_params=_cparams("parallel"),
        name="in_proj",
    )(x, *ws)


def _swa_kernel(q_ref, kvc_ref, kp_ref, vp_ref, sink_ref, o_ref, *cache_refs, prompt, qb, nsub):
    kvc = kvc_ref[...]
    kp = kp_ref[...]
    vp = vp_ref[...]
    nkeys = WINDOW + qb
    if prompt:
        row = lax.broadcasted_iota(jnp.int32, (qb, nkeys), 0)
        col = lax.broadcasted_iota(jnp.int32, (qb, nkeys), 1)
        behind = row // CHUNK - (col // CHUNK - WINDOW // CHUNK)
        band = (behind >= 0) & (behind <= WINDOW // CHUNK)
        first_valid = jnp.where(pl.program_id(1) == 0, WINDOW, 0)
        band_first = band & (col >= first_valid)
    cols = [slice(h * A_HEAD_DIM, (h + 1) * A_HEAD_DIM) for h in range(A_HEADS)]
    units = []
    for sub in range(nsub):
        rows = slice(sub * qb, (sub + 1) * qb)
        before = slice((sub - 1) * qb, sub * qb)
        for kv in range(A_KV_HEADS):
            ks = slice(kv * A_HEAD_DIM, (kv + 1) * A_HEAD_DIM)
            vs = slice(A_KV_WIDTH + kv * A_HEAD_DIM, A_KV_WIDTH + (kv + 1) * A_HEAD_DIM)
            k_prev, v_prev = (kp[:, ks], vp[:, ks]) if sub == 0 else (kvc[before, ks], kvc[before, vs])
            kk = jnp.concatenate([k_prev, kvc[rows, ks]], axis=0).astype(BF16)
            vv = jnp.concatenate([v_prev, kvc[rows, vs]], axis=0).astype(BF16)
            units += [(sub, kv * A_GROUP + g, rows, kk, vv) for g in range(A_GROUP)]
    scores = [_mm_nt(q_ref[rows, cols[h]], kk) for _, h, rows, kk, _ in units]
    es, dens = [], []
    for (sub, h, _, _, _), s in zip(units, scores):
        if prompt:
            s = jnp.where(band_first if sub == 0 else band, s, NEG_BIG)
        sink = sink_ref[h]
        m = jnp.maximum(jnp.max(s, axis=-1, keepdims=True), sink)
        e = jnp.exp(s - m)
        es.append(e.astype(BF16))
        dens.append(jnp.sum(e, axis=-1, keepdims=True) + jnp.exp(sink - m))
    outs = [_mm(e, vv) for e, (_, _, _, _, vv) in zip(es, units)]
    for (_, h, rows, _, _), o, den in zip(units, outs, dens):
        o_ref[rows, cols[h]] = (o / den).astype(o_ref.dtype)
    if not prompt:
        nk_ref, nv_ref = cache_refs
        keep = WINDOW - qb
        nk_ref[0:keep, :] = kp[qb:, :]
        nk_ref[keep:, :] = kvc[:, 0:A_KV_WIDTH]
        nv_ref[0:keep, :] = vp[qb:, :]
        nv_ref[keep:, :] = kvc[:, A_KV_WIDTH:]


def _swa(sq, skv, k_cache, v_cache, sink, prompt):
    bsz, t, _ = sq.shape
    if prompt:
        qb = WINDOW
        nsub = 2 if t % (2 * qb) == 0 else 1
        assert t % (nsub * qb) == 0
        kp_spec = pl.BlockSpec((None, WINDOW, A_KV_WIDTH), lambda b, i: (b, jnp.maximum(i * nsub - 1, 0), 0))
        vp_spec = pl.BlockSpec((None, WINDOW, A_KV_WIDTH), lambda b, i: (b, jnp.maximum(i * nsub - 1, 0), 1))
        kp_arr, vp_arr = skv, skv
        out_shape = [jax.ShapeDtypeStruct((bsz, t, A_WIDTH), BF16)]
        out_specs = [pl.BlockSpec((None, nsub * qb, A_WIDTH), lambda b, i: (b, i, 0))]
    else:
        qb = t
        nsub = 1
        assert t <= WINDOW and t % (2 * SUBLANES) == 0 and k_cache.shape[1] == WINDOW
        kp_spec = pl.BlockSpec((None, WINDOW, A_KV_WIDTH), lambda b, i: (b, 0, 0))
        vp_spec = kp_spec
        kp_arr = k_cache.reshape(bsz, WINDOW, A_KV_WIDTH)
        vp_arr = v_cache.reshape(bsz, WINDOW, A_KV_WIDTH)
        out_shape = [jax.ShapeDtypeStruct((bsz, t, A_WIDTH), BF16),
                     jax.ShapeDtypeStruct((bsz, WINDOW, A_KV_WIDTH), F32),
                     jax.ShapeDtypeStruct((bsz, WINDOW, A_KV_WIDTH), F32)]
        cache_spec = pl.BlockSpec((None, WINDOW, A_KV_WIDTH), lambda b, i: (b, 0, 0))
        out_specs = [pl.BlockSpec((None, qb, A_WIDTH), lambda b, i: (b, i, 0)), cache_spec, cache_spec]
    outs = pl.pallas_call(
        functools.partial(_swa_kernel, prompt=prompt, qb=qb, nsub=nsub),
        grid=(bsz, t // (nsub * qb)),
        in_specs=[pl.BlockSpec((None, nsub * qb, A_WIDTH), lambda b, i: (b, i, 0)),
                  pl.BlockSpec((None, nsub * qb, 2 * A_KV_WIDTH), lambda b, i: (b, i, 0)),
                  kp_spec, vp_spec,
                  pl.BlockSpec(memory_space=pltpu.SMEM)],
        out_specs=out_specs,
        out_shape=out_shape,
        compiler_params=_cparams("parallel", "parallel"),
        name="swa",
    )(sq, skv, kp_arr, vp_arr, sink)
    if prompt:
        new_k = skv[:, t - WINDOW:, 0:A_KV_WIDTH]
        new_v = skv[:, t - WINDOW:, A_KV_WIDTH:]
    else:
        new_k, new_v = outs[1], outs[2]
    shape = (bsz, WINDOW, A_KV_HEADS, A_HEAD_DIM)
    return outs[0], new_k.reshape(shape), new_v.reshape(shape)


def _hgrn_kernel(x_ref, lb_ref, w_ref, s0_ref, o_ref, s_ref, st_ref, *, c):
    tb, tt, _ = x_ref.shape
    ti = pl.program_id(1)

    @pl.when(ti == 0)
    def _():
        for b in range(tb):
            for h in range(B_HEADS):
                st_ref[b, h] = s0_ref[b, h].T

    pos = lax.broadcasted_iota(jnp.int32, (tt, B_DIM), 0)
    r = lax.broadcasted_iota(jnp.int32, (c, c), 0)
    s = lax.broadcasted_iota(jnp.int32, (c, c), 1)
    x = r ^ s
    sizes = [1 << i for i in range(int(math.log2(c)))]
    levels = {hh: (r > s) & (x >= hh) & (x < 2 * hh) for hh in sizes}
    uppers = {hh: (pos % (2 * hh)) >= hh for hh in sizes}
    heads = []
    for b in range(tb):
        for h in range(B_HEADS):
            cs = slice(h * B_DIM, (h + 1) * B_DIM)
            q = x_ref[b, :, cs]
            sig = x_ref[b, :, B_WIDTH + h * B_DIM:B_WIDTH + (h + 1) * B_DIM]
            v = x_ref[b, :, 2 * B_WIDTH + h * B_DIM:2 * B_WIDTH + (h + 1) * B_DIM]
            gate = x_ref[b, :, 3 * B_WIDTH + h * B_DIM:3 * B_WIDTH + (h + 1) * B_DIM]
            lb = lb_ref[:, cs]
            f = jnp.maximum(lb + (1.0 - lb) * sig, F_MIN)
            k = (1.0 - lb) * (1.0 - sig)
            total = f
            qd = q * f
            kd = k
            att = [jnp.where(r == s, jnp.sum((q * k)[j * c:(j + 1) * c], axis=-1, keepdims=True), 0.0)
                   for j in range(tt // c)]
            hh = 1
            while hh < c:
                qh = qd.astype(BF16)
                kh = kd.astype(BF16)
                for j in range(tt // c):
                    rows = slice(j * c, (j + 1) * c)
                    att[j] = jnp.where(levels[hh], _mm_nt(qh[rows], kh[rows]), att[j])
                below = pltpu.roll(total, hh, axis=0)
                above = pltpu.roll(total, tt - hh, axis=0)
                qd = jnp.where(uppers[hh], qd * below, qd)
                kd = jnp.where(uppers[hh], kd, kd * above)
                total = total * jnp.where(uppers[hh], below, above)
                hh *= 2
            heads.append(dict(b=b, h=h, cs=cs, att=att, qc=qd.astype(BF16), kc=kd.astype(BF16), v=v.astype(BF16),
                              gate=gate, total=total))
    chunks = [slice(j * c, (j + 1) * c) for j in range(tt // c)]
    for hd in heads:
        hd["av"] = [_mm(hd["att"][j], hd["v"][rows]) for j, rows in enumerate(chunks)]
        hd["kv"] = [_mm_tn(hd["v"][rows], hd["kc"][rows]) for rows in chunks]
    for j, rows in enumerate(chunks):
        for hd in heads:
            b, h = hd["b"], hd["h"]
            st = st_ref[b, h]
            o = _mm_nt(hd["qc"][rows], st) + hd["av"][j]
            st_ref[b, h] = st * hd["total"][(j + 1) * c - 1:(j + 1) * c, :] + hd["kv"][j]
            o_ref[b, rows, hd["cs"]] = (_rms_scaled(o, w_ref[...]) * hd["gate"][rows]).astype(o_ref.dtype)

    @pl.when(ti == pl.num_programs(1) - 1)
    def _():
        for b in range(tb):
            for h in range(B_HEADS):
                s_ref[b, h] = st_ref[b, h].T


def _layer_state(layer, tb, *dims):
    zeros = (0,) * len(dims)
    return pl.BlockSpec((None, tb) + dims, lambda b, i: (layer, b) + zeros)


def _hgrn(sb, lb, norm_w, state, sl, tb, tt):
    bsz, t, _ = sb.shape
    c = min(CHUNK, t)
    assert tt % c == 0 and t % tt == 0 and bsz % tb == 0
    st_spec = pl.BlockSpec((tb, B_HEADS, B_DIM, B_DIM), lambda b, i: (b, 0, 0, 0))
    return pl.pallas_call(
        functools.partial(_hgrn_kernel, c=c),
        grid=(bsz // tb, t // tt),
        in_specs=[pl.BlockSpec((tb, tt, 4 * B_WIDTH), lambda b, i: (b, i, 0)),
                  pl.BlockSpec((1, B_WIDTH), lambda b, i: (0, 0)),
                  pl.BlockSpec((1, B_DIM), lambda b, i: (0, 0)),
                  _layer_state(sl, tb, B_HEADS, B_DIM, B_DIM)],
        out_specs=[pl.BlockSpec((tb, tt, B_WIDTH), lambda b, i: (b, i, 0)), st_spec],
        out_shape=[jax.ShapeDtypeStruct((bsz, t, B_WIDTH), BF16),
                   jax.ShapeDtypeStruct((bsz, B_HEADS, B_DIM, B_DIM), F32)],
        scratch_shapes=[pltpu.VMEM((tb, B_HEADS, B_DIM, B_DIM), F32)],
        compiler_params=_cparams("parallel", "arbitrary"),
        name="hgrn",
    )(sb, lb.reshape(1, B_WIDTH), norm_w.reshape(1, B_DIM), state)


def _gdn_prep_kernel(x_ref, sm_ref, prev_ref, b0_ref, cw_ref, alog_ref, dtb_ref,
                     sv_ref, sk_ref, qk_ref, qt_ref, kt_ref, ec_ref, cbuf_ref, *, c):
    tb, tt, _ = x_ref.shape
    pad = SUBLANES
    keep = C_CONV - 1
    ti = pl.program_id(1)

    @pl.when(ti == 0)
    def _():
        cbuf_ref[:, pad - keep:pad, :] = b0_ref[...]

    @pl.when(ti > 0)
    def _():
        cbuf_ref[:, pad - keep:pad, :] = prev_ref[:, pad - keep:pad, :]

    cbuf_ref[:, 0:pad - keep, :] = jnp.zeros((tb, pad - keep, C_QKV), F32)
    cbuf_ref[:, pad:pad + tt, :] = x_ref[...]
    win = cbuf_ref[...].reshape(tb * (pad + tt), C_QKV)
    u = win * cw_ref[keep:keep + 1, :]
    for sft in range(1, C_CONV):
        u = u + _shift_rows(win, sft) * cw_ref[keep - sft:keep - sft + 1, :]
    u = _silu(u.reshape(tb, pad + tt, C_QKV)[:, pad:, :])

    small = sm_ref[...]
    pre = small + dtb_ref[...]
    softplus = jnp.maximum(pre, 0.0) + jnp.log(1.0 + jnp.exp(-jnp.abs(pre)))
    g_all = -jnp.exp(alog_ref[...]) * softplus
    beta_all = _sigmoid(small)
    tr = lax.broadcasted_iota(jnp.int32, (tt, tt), 0)
    ts = lax.broadcasted_iota(jnp.int32, (tt, tt), 1)
    chunk_tri = jnp.where((tr >= ts) & (tr // c == ts // c), 1.0, 0.0).astype(F32)
    sel_r = lax.broadcasted_iota(jnp.int32, (SUBLANES, LANES), 0)
    sel_c = lax.broadcasted_iota(jnp.int32, (SUBLANES, LANES), 1)
    sel = jnp.where(sel_r == sel_c, 1.0, 0.0).astype(F32)
    r = lax.broadcasted_iota(jnp.int32, (c, c), 0)
    s = lax.broadcasted_iota(jnp.int32, (c, c), 1)
    incl = r >= s
    units = []
    for b in range(tb):
        cum_all = _mm_f32(chunk_tri, g_all[b])
        cum_rows = _mm_nt_f32(sel, cum_all)
        ecum_all = jnp.exp(cum_all)
        ec_ref[b] = ecum_all
        for h in range(C_HEADS):
            cs = slice(h * C_DIM, (h + 1) * C_DIM)
            uq = u[b, :, cs]
            uk = u[b, :, C_WIDTH + h * C_DIM:C_WIDTH + (h + 1) * C_DIM]
            v = u[b, :, 2 * C_WIDTH + h * C_DIM:2 * C_WIDTH + (h + 1) * C_DIM]
            q = uq * lax.rsqrt(jnp.sum(uq * uq, axis=-1, keepdims=True) + 1e-6) * (C_DIM ** -0.5)
            k = uk * lax.rsqrt(jnp.sum(uk * uk, axis=-1, keepdims=True) + 1e-6)
            beta = beta_all[b, :, h:h + 1]
            cum = cum_all[:, C_HEADS + h:C_HEADS + h + 1]
            ecum = ecum_all[:, C_HEADS + h:C_HEADS + h + 1]
            qt_ref[b, :, cs] = (q * ecum).astype(BF16)
            rhs = jnp.concatenate([v * beta, k * (beta * ecum)], axis=-1)
            for j in range(tt // c):
                rows = slice(j * c, (j + 1) * c)
                cum_j = cum[rows]
                cum_row = cum_rows[C_HEADS + h:C_HEADS + h + 1, rows]
                dec = jnp.where(incl, jnp.exp(jnp.where(incl, cum_j - cum_row, 0.0)), 0.0)
                k_j = k[rows]
                kb = k_j.astype(BF16)
                last = cum_j[c - 1:c, :]
                kt_ref[b, rows, cs] = (k_j * jnp.exp(last - cum_j)).astype(BF16)
                units.append(dict(b=b, rows=rows, cs=cs, qcols=slice(h * c, (h + 1) * c), dec=dec, kb=kb,
                                  qb=q[rows].astype(BF16), beta=beta[rows], rhs=rhs[rows]))
    for un in units:
        un["kk"] = _mm_nt(un["kb"], un["kb"])
        un["qk"] = _mm_nt(un["qb"], un["kb"])
    for un in units:
        qk_ref[un["b"], un["rows"], un["qcols"]] = (un["qk"] * un["dec"]).astype(BF16)
        n = jnp.where(r > s, un["beta"] * un["kk"] * un["dec"], 0.0)
        un["p"] = n
        un["corr"] = -n
    for _ in range(int(math.log2(c)) - 1):
        for un in units:
            pb = un["p"].astype(BF16)
            un["p"] = jnp.dot(pb, pb, preferred_element_type=F32)
        for un in units:
            un["corr"] = un["corr"] + un["p"] + _mm(un["corr"], un["p"])
    for un in units:
        un["sol"] = un["rhs"] + _mm(un["corr"], un["rhs"])
    for un in units:
        sv_ref[un["b"], un["rows"], un["cs"]] = un["sol"][:, 0:C_DIM]
        sk_ref[un["b"], un["rows"], un["cs"]] = un["sol"][:, C_DIM:].astype(BF16)


def _gdn_scan_kernel(sv_ref, sk_ref, qk_ref, qt_ref, kt_ref, ec_ref, z_ref, w_ref, s0_ref,
                     o_ref, s_ref, st_ref, *, c):
    tb, tt, _ = sv_ref.shape
    ti = pl.program_id(1)

    @pl.when(ti == 0)
    def _():
        st_ref[...] = s0_ref[...]

    heads = [(b, h, slice(h * C_DIM, (h + 1) * C_DIM)) for b in range(tb) for h in range(C_HEADS)]
    for j in range(tt // c):
        rows = slice(j * c, (j + 1) * c)
        st = [st_ref[b, h] for b, h, _ in heads]
        stb = [a.astype(BF16) for a in st]
        sks = [_mm(sk_ref[b, rows, cs], stb[i]) for i, (b, h, cs) in enumerate(heads)]
        qs = [_mm(qt_ref[b, rows, cs], stb[i]) for i, (b, h, cs) in enumerate(heads)]
        wb = [(sv_ref[b, rows, cs] - sks[i]).astype(BF16) for i, (b, h, cs) in enumerate(heads)]
        upd = [_mm_tn(kt_ref[b, rows, cs], wb[i]) for i, (b, h, cs) in enumerate(heads)]
        qkw = [_mm(qk_ref[b, rows, h * c:(h + 1) * c], wb[i]) for i, (b, h, cs) in enumerate(heads)]
        for i, (b, h, cs) in enumerate(heads):
            e_last = ec_ref[b, (j + 1) * c - 1:(j + 1) * c, C_HEADS + h:C_HEADS + h + 1]
            st_ref[b, h] = e_last * st[i] + upd[i]
            o_ref[b, rows, cs] = (_rms_scaled(qs[i] + qkw[i], w_ref[...]) * z_ref[b, rows, cs]).astype(o_ref.dtype)

    @pl.when(ti == pl.num_programs(1) - 1)
    def _():
        s_ref[...] = st_ref[...]


def _gdn(sc, conv_w, a_log, dt_bias, norm_w, state, buf, sl, tb_prep, tb_scan, tt):
    bsz, t, _ = sc.shape
    c = min(CHUNK, t)
    keep = C_CONV - 1
    assert t >= keep and tt % c == 0 and t % tt == 0
    pad_vec = lambda a: jnp.zeros((1, LANES), F32).at[0, C_HEADS:2 * C_HEADS].set(a.astype(F32))
    const = lambda shape: pl.BlockSpec(shape, lambda b, i: (0,) * len(shape))
    tile = lambda tb, w, col=0: pl.BlockSpec((tb, tt, w), lambda b, i: (b, i, col))
    shape = lambda w, dt: jax.ShapeDtypeStruct((bsz, t, w), dt)
    small_col = (C_QKV + C_WIDTH) // LANES
    prev_spec = pl.BlockSpec((tb_prep, SUBLANES, C_QKV),
                             lambda b, i: (b, jnp.maximum(i * (tt // SUBLANES) - 1, 0), 0))
    sv, sk, qk, qt, kt, ec = pl.pallas_call(
        functools.partial(_gdn_prep_kernel, c=c),
        grid=(bsz // tb_prep, t // tt),
        in_specs=[tile(tb_prep, C_QKV), tile(tb_prep, LANES, small_col), prev_spec,
                  _layer_state(sl, tb_prep, keep, C_QKV),
                  const((C_CONV, C_QKV)), const((1, LANES)), const((1, LANES))],
        out_specs=[tile(tb_prep, C_WIDTH), tile(tb_prep, C_WIDTH), tile(tb_prep, C_HEADS * c),
                   tile(tb_prep, C_WIDTH), tile(tb_prep, C_WIDTH), tile(tb_prep, LANES)],
        out_shape=[shape(C_WIDTH, F32), shape(C_WIDTH, BF16), shape(C_HEADS * c, BF16),
                   shape(C_WIDTH, BF16), shape(C_WIDTH, BF16), shape(LANES, F32)],
        scratch_shapes=[pltpu.VMEM((tb_prep, SUBLANES + tt, C_QKV), F32)],
        compiler_params=_cparams("parallel", "parallel"),
        name="gdn_prep",
    )(sc, sc, sc, buf, conv_w, pad_vec(a_log), pad_vec(dt_bias))
    st_spec = pl.BlockSpec((tb_scan, C_HEADS, C_DIM, C_DIM), lambda b, i: (b, 0, 0, 0))
    o, new_state = pl.pallas_call(
        functools.partial(_gdn_scan_kernel, c=c),
        grid=(bsz // tb_scan, t // tt),
        in_specs=[tile(tb_scan, C_WIDTH), tile(tb_scan, C_WIDTH), tile(tb_scan, C_HEADS * c),
                  tile(tb_scan, C_WIDTH), tile(tb_scan, C_WIDTH), tile(tb_scan, LANES),
                  tile(tb_scan, C_WIDTH, C_QKV // C_WIDTH), const((1, C_DIM)),
                  _layer_state(sl, tb_scan, C_HEADS, C_DIM, C_DIM)],
        out_specs=[tile(tb_scan, C_WIDTH), st_spec],
        out_shape=[shape(C_WIDTH, BF16), jax.ShapeDtypeStruct((bsz, C_HEADS, C_DIM, C_DIM), F32)],
        scratch_shapes=[pltpu.VMEM((tb_scan, C_HEADS, C_DIM, C_DIM), F32)],
        compiler_params=_cparams("parallel", "arbitrary"),
        name="gdn_scan",
    )(sv, sk, qk, qt, kt, ec, sc, norm_w.reshape(1, C_DIM), state)
    return o, new_state, sc[:, t - keep:, 0:C_QKV]


def _merge_rows(oa, ob, oc, gates, x, wa, wb, wc, wo, ln_g, ln_b, alpha):
    d = x.shape[-1]
    mix = gates[:, 0:d].astype(F32) * _mm(oa, wa)
    mix = mix + gates[:, d:2 * d].astype(F32) * _mm(ob, wb)
    mix = mix + gates[:, 2 * d:3 * d].astype(F32) * _mm(oc, wc)
    return _ln_rows(alpha * x + _mm(mix, wo), ln_g, ln_b)


def _ffn_kernel(oa_ref, ob_ref, oc_ref, g_ref, x_ref, wa_ref, wb_ref, wc_ref, wo_ref, l1g_ref, l1b_ref,
                b0_ref, wu_ref, cw_ref, cb_ref, wd_ref, lg_ref, lb_ref,
                yf_ref, yb_ref, b_ref, carry_ref, act_ref, *, alpha):
    tb, tt, d = x_ref.shape
    d_ff = wd_ref.shape[0]
    m = tb * tt
    nj = d_ff // FFN_COLS
    keep = F_CONV - 1
    pad = SUBLANES
    ti = pl.program_id(1)

    @pl.when(ti == 0)
    def _():
        carry_ref[:, 0:pad - keep, :] = jnp.zeros((tb, pad - keep, carry_ref.shape[-1]), F32)
        carry_ref[:, pad - keep:, :] = b0_ref[...]

    flat = lambda ref: ref[...].reshape(m, ref.shape[-1])
    x = _merge_rows(flat(oa_ref), flat(ob_ref), flat(oc_ref), flat(g_ref), flat(x_ref),
                    wa_ref[...], wb_ref[...], wc_ref[...], wo_ref[...], l1g_ref[...], l1b_ref[...], alpha)
    xb = x.astype(BF16)

    def up(c0):
        return jnp.dot(xb, wu_ref[:, c0:c0 + FFN_COLS], preferred_element_type=F32)

    def conv(c0, rows):
        cols = slice(c0, c0 + FFN_COLS)
        rows = rows.reshape(tb, tt, FFN_COLS)
        win = jnp.concatenate([carry_ref[:, :, cols], rows], axis=1).reshape(tb * (pad + tt), FFN_COLS)
        out = win * cw_ref[keep:keep + 1, cols]
        for sft in range(1, F_CONV):
            out = out + _shift_rows(win, sft) * cw_ref[keep - sft:keep - sft + 1, cols]
        carry_ref[:, pad - keep:, cols] = rows[:, tt - keep:, :]
        return out.reshape(tb, pad + tt, FFN_COLS)[:, pad:, :] + cb_ref[:, cols]

    parts = [0, (nj + 2) // 3, (2 * nj + 2) // 3, nj]
    down = None
    nxt = (up(0), up(d_ff))
    for j in range(nj):
        cur = nxt
        if j + 1 < nj:
            nxt = (up((j + 1) * FFN_COLS), up(d_ff + (j + 1) * FFN_COLS))
        gate = conv(j * FFN_COLS, cur[0])
        val = conv(d_ff + j * FFN_COLS, cur[1])
        act_ref[:, j * FFN_COLS:(j + 1) * FFN_COLS] = (_silu(gate) * val).reshape(m, FFN_COLS).astype(BF16)
        if j + 1 in parts:
            lo = parts[parts.index(j + 1) - 1] * FFN_COLS
            part = jnp.dot(act_ref[:, lo:(j + 1) * FFN_COLS], wd_ref[lo:(j + 1) * FFN_COLS, :],
                           preferred_element_type=F32)
            down = part if down is None else down + part
    y = _ln_rows(alpha * x + down, lg_ref[...], lb_ref[...]).reshape(tb, tt, d)
    yf_ref[...] = y
    yb_ref[...] = y.astype(BF16)

    @pl.when(ti == pl.num_programs(1) - 1)
    def _():
        b_ref[...] = carry_ref[:, pad - keep:, :]


def _merge_ffn(oa, ob, oc, gates, x, w_branch, wout, ln1_g, ln1_b,
               buf, sl, wu, cw, cb, wd, layer, ln_g, ln_b, alpha, tb, tt):
    bsz, t, d = x.shape
    d_ff = wd.shape[1]
    keep = F_CONV - 1
    assert A_WIDTH == B_WIDTH == C_WIDTH
    tile = lambda w: pl.BlockSpec((tb, tt, w), lambda b, i: (b, i, 0))
    row = tile(d)
    branch = lambda k: _resident(w_branch, layer, row_block=k, rows=A_WIDTH)
    buf_spec = pl.BlockSpec((tb, keep, 2 * d_ff), lambda b, i: (b, 0, 0))
    const = lambda a: pl.BlockSpec(a.shape, lambda b, i: (0,) * a.ndim, pipeline_mode=pl.Buffered(1))
    l1g = ln1_g.reshape(1, d)
    l1b = ln1_b.reshape(1, d)
    lg = ln_g.reshape(1, d)
    lb = ln_b.reshape(1, d)
    cb = cb.reshape(1, 2 * d_ff)
    return pl.pallas_call(
        functools.partial(_ffn_kernel, alpha=alpha),
        grid=(bsz // tb, t // tt),
        in_specs=[tile(A_WIDTH), tile(B_WIDTH), tile(C_WIDTH), tile(N_BRANCH * d), row,
                  branch(0), branch(1), branch(2), _resident(wout, layer), const(l1g), const(l1b),
                  _layer_state(sl, tb, keep, 2 * d_ff), _resident(wu, layer), const(cw), const(cb),
                  _resident(wd, layer),
                  const(lg), const(lb)],
        out_specs=[row, row, buf_spec],
        out_shape=[jax.ShapeDtypeStruct((bsz, t, d), F32),
                   jax.ShapeDtypeStruct((bsz, t, d), BF16),
                   jax.ShapeDtypeStruct((bsz, keep, 2 * d_ff), F32)],
        scratch_shapes=[pltpu.VMEM((tb, SUBLANES, 2 * d_ff), F32),
                        pltpu.VMEM((tb * tt, d_ff), BF16)],
        compiler_params=_cparams("parallel", "arbitrary"),
        name="merge_ffn",
    )(oa, ob, oc, gates, x, w_branch, w_branch, w_branch, wout, l1g, l1b, buf, wu, cw, cb, wd, lg, lb)


def _layer(xf, xb, state, sl, p, prompt, alpha):
    k_cache, v_cache, s_hgrn, s_gdn, buf_gdn, buf_ffn = state
    bsz, t, d = xf.shape
    n = bsz * t
    x2 = xb.reshape(n, d)
    sq, skv, sb, sc, sg = _in_proj(x2, p["w_in"], p["layer"])
    sq, skv, sb, sc = (a.reshape(bsz, t, -1) for a in (sq, skv, sb, sc))
    oa, new_k, new_v = _swa(sq, skv, k_cache, v_cache, p["sink"], prompt)
    ob, new_hgrn = _hgrn(sb, p["lb"], p["hgrn_w"], s_hgrn, sl, *((1, 256) if prompt else (8, t)))
    gdn_tiles = (1, bsz, 256) if prompt else (8, 8, t)
    oc, new_gdn, new_gconv = _gdn(sc, p["gdn_conv_w"], p["a_log"], p["dt_bias"], p["gdn_w"], s_gdn, buf_gdn,
                                  sl, *gdn_tiles)
    tb, tt = (1, 256) if prompt else (8, t)
    zf, zb, new_fconv = _merge_ffn(oa, ob, oc, sg.reshape(bsz, t, -1), xf, p["w_branch"], p["wout"],
                                   p["ln1_g"], p["ln1_b"], buf_ffn, sl, p["wu"], p["fcw"], p["fcb"], p["wd"],
                                   p["layer"], p["ln2_g"], p["ln2_b"], alpha, tb, tt)
    return zf, zb, (new_k, new_v, new_hgrn, new_gdn, new_gconv, new_fconv)


def kernel(x_prompt, x_sample, cache_swa_k, cache_swa_v, state_hgrn, state_gdn, state_gdn_conv, state_ffn_conv, ln_in_g, ln_in_b, w_in, attn_sinks, hgrn_lb_logits, hgrn_norm_w, gdn_conv_w, gdn_a_log, gdn_dt_bias, gdn_norm_w, w_branch, w_out, ln1_g, ln1_b, w_up, ffn_conv_w, ffn_conv_b, w_down, ln2_g, ln2_b):
    depth, d, _ = w_in.shape
    alpha = (2 * depth) ** 0.25
    lb_sm = jax.nn.softmax(hgrn_lb_logits.astype(F32), axis=0)
    lower_bounds = jnp.cumsum(lb_sm, axis=0) - lb_sm[0]

    o_a = A_WIDTH + 2 * A_KV_WIDTH
    o_b = o_a + 4 * B_WIDTH
    o_c = o_b + C_QKV + C_WIDTH
    o_s = o_c + 2 * C_HEADS
    wq = w_in[:, :, 0:A_WIDTH].astype(BF16)
    wkv = w_in[:, :, A_WIDTH:o_a].astype(BF16)
    wb = w_in[:, :, o_a:o_b].astype(BF16)
    wc = jnp.concatenate([w_in[:, :, o_b:o_s], jnp.zeros((depth, d, LANES - 2 * C_HEADS), w_in.dtype)],
                         axis=-1).astype(BF16)
    wg = w_in[:, :, o_s:].astype(BF16)
    w_in_groups = (wq, wkv, wb, wc, wg)
    w_branch_b = w_branch.astype(BF16)
    wout = w_out.astype(BF16)
    wu, fcw, fcb, wd = w_up.astype(BF16), ffn_conv_w.astype(F32), ffn_conv_b.astype(F32), w_down.astype(BF16)

    hp_f, hp_b = _layer_norm(x_prompt, ln_in_g, ln_in_b)
    hs_f, hs_b = _layer_norm(x_sample, ln_in_g, ln_in_b)
    bp = x_prompt.shape[0]
    prompt_state = (None, None,
                    jnp.zeros((1, bp, B_HEADS, B_DIM, B_DIM), F32),
                    jnp.zeros((1, bp, C_HEADS, C_DIM, C_DIM), F32),
                    jnp.zeros((1, bp, C_CONV - 1, C_QKV), F32),
                    jnp.zeros((1, bp, F_CONV - 1, w_up.shape[-1]), F32))
    p_new, s_new = [], []
    for l in range(depth):
        p = dict(layer=l, w_in=w_in_groups, sink=attn_sinks[l].astype(F32), lb=lower_bounds[l],
                 hgrn_w=hgrn_norm_w[l], gdn_conv_w=gdn_conv_w[l], a_log=gdn_a_log[l], dt_bias=gdn_dt_bias[l],
                 gdn_w=gdn_norm_w[l], w_branch=w_branch_b, wout=wout,
                 ln1_g=ln1_g[l], ln1_b=ln1_b[l], wu=wu, fcw=fcw[l], fcb=fcb[l], wd=wd,
                 ln2_g=ln2_g[l], ln2_b=ln2_b[l])
        hp_f, hp_b, st_p = _layer(hp_f, hp_b, prompt_state, 0, p, True, alpha)
        sample_state = (cache_swa_k[l], cache_swa_v[l], state_hgrn, state_gdn, state_gdn_conv, state_ffn_conv)
        hs_f, hs_b, st_s = _layer(hs_f, hs_b, sample_state, l, p, False, alpha)
        p_new.append(st_p)
        s_new.append(st_s)

    def stack(states, i):
        return jnp.stack([st[i] for st in states])

    return (hp_f, hs_f) + tuple(stack(p_new, i) for i in range(6)) + tuple(stack(s_new, i) for i in range(6))
```

```python
import functools
import math

import jax
import jax.numpy as jnp
from jax import lax
from jax.experimental import pallas as pl
from jax.experimental.pallas import tpu as pltpu

F32 = jnp.float32
BF16 = jnp.bfloat16

CHUNK = 64
WINDOW = 128
A_HEADS = 8
A_KV_HEADS = 2
A_GROUP = A_HEADS // A_KV_HEADS
A_HEAD_DIM = 64
A_SCALE = A_HEAD_DIM ** -0.5
A_WIDTH = A_HEADS * A_HEAD_DIM
A_KV_WIDTH = A_KV_HEADS * A_HEAD_DIM
B_HEADS = 4
B_DIM = 128
C_HEADS = 4
C_DIM = 128
C_CONV = 4
F_CONV = 3
N_BRANCH = 3
B_WIDTH = B_HEADS * B_DIM
C_WIDTH = C_HEADS * C_DIM
C_QKV = 3 * C_WIDTH
LN_EPS = 1e-5
RMS_EPS = 1e-6
NEG_BIG = -1e30
F_MIN = 1e-30

LANES = 128
SUBLANES = 8
FFN_COLS = 256
VMEM_LIMIT = 48 * 1024 * 1024


def _cparams(*sem):
    return pltpu.CompilerParams(dimension_semantics=sem, vmem_limit_bytes=VMEM_LIMIT)


def _resident(a, layer, row_block=None, rows=None):
    shape = (None, a.shape[1] if rows is None else rows, a.shape[2])
    rb = 0 if row_block is None else row_block
    return pl.BlockSpec(shape, lambda *_: (layer, rb, 0), pipeline_mode=pl.Buffered(1))


def _mm(a, b):
    return jnp.dot(a.astype(BF16), b.astype(BF16), preferred_element_type=F32)


def _mm_nt(a, b):
    return lax.dot_general(a.astype(BF16), b.astype(BF16), (((1,), (1,)), ((), ())),
                           preferred_element_type=F32)


def _mm_tn(a, b):
    return lax.dot_general(a.astype(BF16), b.astype(BF16), (((0,), (0,)), ((), ())),
                           preferred_element_type=F32)


def _mm_f32(a, b):
    return jnp.dot(a, b, preferred_element_type=F32, precision=lax.Precision.HIGHEST)


def _mm_nt_f32(a, b):
    return lax.dot_general(a, b, (((1,), (1,)), ((), ())), preferred_element_type=F32,
                           precision=lax.Precision.HIGHEST)


def _sigmoid(x):
    return 1.0 / (1.0 + jnp.exp(-x))


def _silu(x):
    return x * _sigmoid(x)


def _ln_rows(x, g, b):
    mu = jnp.mean(x, axis=-1, keepdims=True)
    xc = x - mu
    var = jnp.mean(xc * xc, axis=-1, keepdims=True)
    return xc * lax.rsqrt(var + LN_EPS) * g + b


def _rms_scaled(o, w):
    return o * lax.rsqrt(jnp.mean(o * o, axis=-1, keepdims=True) + RMS_EPS) * w


def _shift_rows(win, sft):
    rows, cols = win.shape
    groups = rows // SUBLANES
    rot = pltpu.roll(win.reshape(groups, SUBLANES, cols), sft, axis=1)
    above = jnp.concatenate([rot[groups - 1:], rot[:groups - 1]], axis=0)
    sub = lax.broadcasted_iota(jnp.int32, rot.shape, 1)
    return jnp.where(sub < sft, above, rot).reshape(rows, cols)


def _tri(c, lower):
    r = lax.broadcasted_iota(jnp.int32, (c, c), 0)
    s = lax.broadcasted_iota(jnp.int32, (c, c), 1)
    return jnp.where(r >= s if lower else r <= s, 1.0, 0.0).astype(F32)


def _ln_kernel(x_ref, g_ref, b_ref, of_ref, ob_ref):
    y = _ln_rows(x_ref[...], g_ref[...], b_ref[...])
    of_ref[...] = y
    ob_ref[...] = y.astype(BF16)


def _layer_norm(x, g, b):
    bsz, t, d = x.shape
    n = bsz * t
    tm = min(512, n)
    of, ob = pl.pallas_call(
        _ln_kernel,
        grid=(n // tm,),
        in_specs=[pl.BlockSpec((tm, d), lambda i: (i, 0)),
                  pl.BlockSpec((1, d), lambda i: (0, 0)),
                  pl.BlockSpec((1, d), lambda i: (0, 0))],
        out_specs=[pl.BlockSpec((tm, d), lambda i: (i, 0)),
                   pl.BlockSpec((tm, d), lambda i: (i, 0))],
        out_shape=[jax.ShapeDtypeStruct((n, d), F32), jax.ShapeDtypeStruct((n, d), BF16)],
        compiler_params=_cparams("parallel"),
        name="input_ln",
    )(x.reshape(n, d), g.reshape(1, d), b.reshape(1, d))
    return of.reshape(bsz, t, d), ob.reshape(bsz, t, d)


def _in_proj_kernel(x_ref, wq_ref, wkv_ref, wb_ref, wc_ref, wg_ref, q_ref, kv_ref, b_ref, c_ref, g_ref):
    x = x_ref[...]
    dot = lambda w_ref: jnp.dot(x, w_ref[...], preferred_element_type=F32)
    q_ref[...] = (dot(wq_ref) * A_SCALE).astype(BF16)
    kv_ref[...] = dot(wkv_ref)
    hb = dot(wb_ref)
    b_ref[:, 0:B_WIDTH] = _silu(hb[:, 0:B_WIDTH])
    b_ref[:, B_WIDTH:2 * B_WIDTH] = _sigmoid(hb[:, B_WIDTH:2 * B_WIDTH])
    b_ref[:, 2 * B_WIDTH:3 * B_WIDTH] = hb[:, 2 * B_WIDTH:3 * B_WIDTH]
    b_ref[:, 3 * B_WIDTH:] = _silu(hb[:, 3 * B_WIDTH:])
    gc = dot(wc_ref)
    c_ref[:, 0:C_QKV] = gc[:, 0:C_QKV]
    c_ref[:, C_QKV:C_QKV + C_WIDTH] = _silu(gc[:, C_QKV:C_QKV + C_WIDTH])
    c_ref[:, C_QKV + C_WIDTH:] = gc[:, C_QKV + C_WIDTH:]
    g_ref[...] = _sigmoid(dot(wg_ref)).astype(BF16)


def _in_proj(x, ws, layer):
    n, k = x.shape
    tm = min(256, n)
    dts = (BF16, F32, F32, F32, BF16)
    return pl.pallas_call(
        _in_proj_kernel,
        grid=(n // tm,),
        in_specs=[pl.BlockSpec((tm, k), lambda i: (i, 0))] + [_resident(w, layer) for w in ws],
        out_specs=[pl.BlockSpec((tm, w.shape[2]), lambda i: (i, 0)) for w in ws],
        out_shape=[jax.ShapeDtypeStruct((n, w.shape[2]), dt) for w, dt in zip(ws, dts)],
        compiler_params=_cparams("parallel"),
        name="in_proj",
    )(x, *ws)


def _swa_kernel(q_ref, kvc_ref, kp_ref, vp_ref, sink_ref, o_ref, *cache_refs, prompt, qb, nsub):
    kvc = kvc_ref[...]
    kp = kp_ref[...]
    vp = vp_ref[...]
    nkeys = WINDOW + qb
    if prompt:
        row = lax.broadcasted_iota(jnp.int32, (qb, nkeys), 0)
        col = lax.broadcasted_iota(jnp.int32, (qb, nkeys), 1)
        behind = row // CHUNK - (col // CHUNK - WINDOW // CHUNK)
        band = (behind >= 0) & (behind <= WINDOW // CHUNK)
        first_valid = jnp.where(pl.program_id(1) == 0, WINDOW, 0)
        band_first = band & (col >= first_valid)
    cols = [slice(h * A_HEAD_DIM, (h + 1) * A_HEAD_DIM) for h in range(A_HEADS)]
    units = []
    for sub in range(nsub):
        rows = slice(sub * qb, (sub + 1) * qb)
        before = slice((sub - 1) * qb, sub * qb)
        for kv in range(A_KV_HEADS):
            ks = slice(kv * A_HEAD_DIM, (kv + 1) * A_HEAD_DIM)
            vs = slice(A_KV_WIDTH + kv * A_HEAD_DIM, A_KV_WIDTH + (kv + 1) * A_HEAD_DIM)
            k_prev, v_prev = (kp[:, ks], vp[:, ks]) if sub == 0 else (kvc[before, ks], kvc[before, vs])
            kk = jnp.concatenate([k_prev, kvc[rows, ks]], axis=0).astype(BF16)
            vv = jnp.concatenate([v_prev, kvc[rows, vs]], axis=0).astype(BF16)
            units += [(sub, kv * A_GROUP + g, rows, kk, vv) for g in range(A_GROUP)]
    scores = [_mm_nt(q_ref[rows, cols[h]], kk) for _, h, rows, kk, _ in units]
    es, dens = [], []
    for (sub, h, _, _, _), s in zip(units, scores):
        if prompt:
            s = jnp.where(band_first if sub == 0 else band, s, NEG_BIG)
        sink = sink_ref[h]
        m = jnp.maximum(jnp.max(s, axis=-1, keepdims=True), sink)
        e = jnp.exp(s - m)
        es.append(e.astype(BF16))
        dens.append(jnp.sum(e, axis=-1, keepdims=True) + jnp.exp(sink - m))
    outs = [_mm(e, vv) for e, (_, _, _, _, vv) in zip(es, units)]
    for (_, h, rows, _, _), o, den in zip(units, outs, dens):
        o_ref[rows, cols[h]] = (o / den).astype(o_ref.dtype)
    if not prompt:
        nk_ref, nv_ref = cache_refs
        keep = WINDOW - qb
        nk_ref[0:keep, :] = kp[qb:, :]
        nk_ref[keep:, :] = kvc[:, 0:A_KV_WIDTH]
        nv_ref[0:keep, :] = vp[qb:, :]
        nv_ref[keep:, :] = kvc[:, A_KV_WIDTH:]


def _swa(sq, skv, k_cache, v_cache, sink, prompt):
    bsz, t, _ = sq.shape
    if prompt:
        qb = WINDOW
        nsub = 2 if t % (2 * qb) == 0 else 1
        assert t % (nsub * qb) == 0
        kp_spec = pl.BlockSpec((None, WINDOW, A_KV_WIDTH), lambda b, i: (b, jnp.maximum(i * nsub - 1, 0), 0))
        vp_spec = pl.BlockSpec((None, WINDOW, A_KV_WIDTH), lambda b, i: (b, jnp.maximum(i * nsub - 1, 0), 1))
        kp_arr, vp_arr = skv, skv
        out_shape = [jax.ShapeDtypeStruct((bsz, t, A_WIDTH), BF16)]
        out_specs = [pl.BlockSpec((None, nsub * qb, A_WIDTH), lambda b, i: (b, i, 0))]
    else:
        qb = t
        nsub = 1
        assert t <= WINDOW and t % (2 * SUBLANES) == 0 and k_cache.shape[1] == WINDOW
        kp_spec = pl.BlockSpec((None, WINDOW, A_KV_WIDTH), lambda b, i: (b, 0, 0))
        vp_spec = kp_spec
        kp_arr = k_cache.reshape(bsz, WINDOW, A_KV_WIDTH)
        vp_arr = v_cache.reshape(bsz, WINDOW, A_KV_WIDTH)
        out_shape = [jax.ShapeDtypeStruct((bsz, t, A_WIDTH), BF16),
                     jax.ShapeDtypeStruct((bsz, WINDOW, A_KV_WIDTH), F32),
                     jax.ShapeDtypeStruct((bsz, WINDOW, A_KV_WIDTH), F32)]
        cache_spec = pl.BlockSpec((None, WINDOW, A_KV_WIDTH), lambda b, i: (b, 0, 0))
        out_specs = [pl.BlockSpec((None, qb, A_WIDTH), lambda b, i: (b, i, 0)), cache_spec, cache_spec]
    outs = pl.pallas_call(
        functools.partial(_swa_kernel, prompt=prompt, qb=qb, nsub=nsub),
        grid=(bsz, t // (nsub * qb)),
        in_specs=[pl.BlockSpec((None, nsub * qb, A_WIDTH), lambda b, i: (b, i, 0)),
                  pl.BlockSpec((None, nsub * qb, 2 * A_KV_WIDTH), lambda b, i: (b, i, 0)),
                  kp_spec, vp_spec,
                  pl.BlockSpec(memory_space=pltpu.SMEM)],
        out_specs=out_specs,
        out_shape=out_shape,
        compiler_params=_cparams("parallel", "parallel"),
        name="swa",
    )(sq, skv, kp_arr, vp_arr, sink)
    if prompt:
        new_k = skv[:, t - WINDOW:, 0:A_KV_WIDTH]
        new_v = skv[:, t - WINDOW:, A_KV_WIDTH:]
    else:
        new_k, new_v = outs[1], outs[2]
    shape = (bsz, WINDOW, A_KV_HEADS, A_HEAD_DIM)
    return outs[0], new_k.reshape(shape), new_v.reshape(shape)


def _hgrn_kernel(x_ref, lb_ref, w_ref, s0_ref, o_ref, s_ref, st_ref, *, c):
    tb, tt, _ = x_ref.shape
    ti = pl.program_id(1)

    @pl.when(ti == 0)
    def _():
        for b in range(tb):
            for h in range(B_HEADS):
                st_ref[b, h] = s0_ref[b, h].T

    pos = lax.broadcasted_iota(jnp.int32, (tt, B_DIM), 0)
    r = lax.broadcasted_iota(jnp.int32, (c, c), 0)
    s = lax.broadcasted_iota(jnp.int32, (c, c), 1)
    x = r ^ s
    sizes = [1 << i for i in range(int(math.log2(c)))]
    levels = {hh: (r > s) & (x >= hh) & (x < 2 * hh) for hh in sizes}
    uppers = {hh: (pos % (2 * hh)) >= hh for hh in sizes}
    heads = []
    for b in range(tb):
        for h in range(B_HEADS):
            cs = slice(h * B_DIM, (h + 1) * B_DIM)
            q = x_ref[b, :, cs]
            sig = x_ref[b, :, B_WIDTH + h * B_DIM:B_WIDTH + (h + 1) * B_DIM]
            v = x_ref[b, :, 2 * B_WIDTH + h * B_DIM:2 * B_WIDTH + (h + 1) * B_DIM]
            gate = x_ref[b, :, 3 * B_WIDTH + h * B_DIM:3 * B_WIDTH + (h + 1) * B_DIM]
            lb = lb_ref[:, cs]
            f = jnp.maximum(lb + (1.0 - lb) * sig, F_MIN)
            k = (1.0 - lb) * (1.0 - sig)
            total = f
            qd = q * f
            kd = k
            att = [jnp.where(r == s, jnp.sum((q * k)[j * c:(j + 1) * c], axis=-1, keepdims=True), 0.0)
                   for j in range(tt // c)]
            hh = 1
            while hh < c:
                qh = qd.astype(BF16)
                kh = kd.astype(BF16)
                for j in range(tt // c):
                    rows = slice(j * c, (j + 1) * c)
                    att[j] = jnp.where(levels[hh], _mm_nt(qh[rows], kh[rows]), att[j])
                below = pltpu.roll(total, hh, axis=0)
                above = pltpu.roll(total, tt - hh, axis=0)
                qd = jnp.where(uppers[hh], qd * below, qd)
                kd = jnp.where(uppers[hh], kd, kd * above)
                total = total * jnp.where(uppers[hh], below, above)
                hh *= 2
            heads.append(dict(b=b, h=h, cs=cs, att=att, qc=qd.astype(BF16), kc=kd.astype(BF16), v=v.astype(BF16),
                              gate=gate, total=total))
    chunks = [slice(j * c, (j + 1) * c) for j in range(tt // c)]
    for hd in heads:
        hd["av"] = [_mm(hd["att"][j], hd["v"][rows]) for j, rows in enumerate(chunks)]
        hd["kv"] = [_mm_tn(hd["v"][rows], hd["kc"][rows]) for rows in chunks]
    for j, rows in enumerate(chunks):
        for hd in heads:
            b, h = hd["b"], hd["h"]
            st = st_ref[b, h]
            o = _mm_nt(hd["qc"][rows], st) + hd["av"][j]
            st_ref[b, h] = st * hd["total"][(j + 1) * c - 1:(j + 1) * c, :] + hd["kv"][j]
            o_ref[b, rows, hd["cs"]] = (_rms_scaled(o, w_ref[...]) * hd["gate"][rows]).astype(o_ref.dtype)

    @pl.when(ti == pl.num_programs(1) - 1)
    def _():
        for b in range(tb):
            for h in range(B_HEADS):
                s_ref[b, h] = st_ref[b, h].T


def _layer_state(layer, tb, *dims):
    zeros = (0,) * len(dims)
    return pl.BlockSpec((None, tb) + dims, lambda b, i: (layer, b) + zeros)


def _hgrn(sb, lb, norm_w, state, sl, tb, tt):
    bsz, t, _ = sb.shape
    c = min(CHUNK, t)
    assert tt % c == 0 and t % tt == 0 and bsz % tb == 0
    st_spec = pl.BlockSpec((tb, B_HEADS, B_DIM, B_DIM), lambda b, i: (b, 0, 0, 0))
    return pl.pallas_call(
        functools.partial(_hgrn_kernel, c=c),
        grid=(bsz // tb, t // tt),
        in_specs=[pl.BlockSpec((tb, tt, 4 * B_WIDTH), lambda b, i: (b, i, 0)),
                  pl.BlockSpec((1, B_WIDTH), lambda b, i: (0, 0)),
                  pl.BlockSpec((1, B_DIM), lambda b, i: (0, 0)),
                  _layer_state(sl, tb, B_HEADS, B_DIM, B_DIM)],
        out_specs=[pl.BlockSpec((tb, tt, B_WIDTH), lambda b, i: (b, i, 0)), st_spec],
        out_shape=[jax.ShapeDtypeStruct((bsz, t, B_WIDTH), BF16),
                   jax.ShapeDtypeStruct((bsz, B_HEADS, B_DIM, B_DIM), F32)],
        scratch_shapes=[pltpu.VMEM((tb, B_HEADS, B_DIM, B_DIM), F32)],
        compiler_params=_cparams("parallel", "arbitrary"),
        name="hgrn",
    )(sb, lb.reshape(1, B_WIDTH), norm_w.reshape(1, B_DIM), state)


def _gdn_prep_kernel(x_ref, sm_ref, prev_ref, b0_ref, cw_ref, alog_ref, dtb_ref,
                     sv_ref, sk_ref, qk_ref, qt_ref, kt_ref, ec_ref, cbuf_ref, *, c):
    tb, tt, _ = x_ref.shape
    pad = SUBLANES
    keep = C_CONV - 1
    ti = pl.program_id(1)

    @pl.when(ti == 0)
    def _():
        cbuf_ref[:, pad - keep:pad, :] = b0_ref[...]

    @pl.when(ti > 0)
    def _():
        cbuf_ref[:, pad - keep:pad, :] = prev_ref[:, pad - keep:pad, :]

    cbuf_ref[:, 0:pad - keep, :] = jnp.zeros((tb, pad - keep, C_QKV), F32)
    cbuf_ref[:, pad:pad + tt, :] = x_ref[...]
    win = cbuf_ref[...].reshape(tb * (pad + tt), C_QKV)
    u = win * cw_ref[keep:keep + 1, :]
    for sft in range(1, C_CONV):
        u = u + _shift_rows(win, sft) * cw_ref[keep - sft:keep - sft + 1, :]
    u = _silu(u.reshape(tb, pad + tt, C_QKV)[:, pad:, :])

    small = sm_ref[...]
    pre = small + dtb_ref[...]
    softplus = jnp.maximum(pre, 0.0) + jnp.log(1.0 + jnp.exp(-jnp.abs(pre)))
    g_all = -jnp.exp(alog_ref[...]) * softplus
    beta_all = _sigmoid(small)
    tr = lax.broadcasted_iota(jnp.int32, (tt, tt), 0)
    ts = lax.broadcasted_iota(jnp.int32, (tt, tt), 1)
    chunk_tri = jnp.where((tr >= ts) & (tr // c == ts // c), 1.0, 0.0).astype(F32)
    sel_r = lax.broadcasted_iota(jnp.int32, (SUBLANES, LANES), 0)
    sel_c = lax.broadcasted_iota(jnp.int32, (SUBLANES, LANES), 1)
    sel = jnp.where(sel_r == sel_c, 1.0, 0.0).astype(F32)
    r = lax.broadcasted_iota(jnp.int32, (c, c), 0)
    s = lax.broadcasted_iota(jnp.int32, (c, c), 1)
    incl = r >= s
    units = []
    for b in range(tb):
        cum_all = _mm_f32(chunk_tri, g_all[b])
        cum_rows = _mm_nt_f32(sel, cum_all)
        ecum_all = jnp.exp(cum_all)
        ec_ref[b] = ecum_all
        for h in range(C_HEADS):
            cs = slice(h * C_DIM, (h + 1) * C_DIM)
            uq = u[b, :, cs]
            uk = u[b, :, C_WIDTH + h * C_DIM:C_WIDTH + (h + 1) * C_DIM]
            v = u[b, :, 2 * C_WIDTH + h * C_DIM:2 * C_WIDTH + (h + 1) * C_DIM]
            q = uq * lax.rsqrt(jnp.sum(uq * uq, axis=-1, keepdims=True) + 1e-6) * (C_DIM ** -0.5)
            k = uk * lax.rsqrt(jnp.sum(uk * uk, axis=-1, keepdims=True) + 1e-6)
            beta = beta_all[b, :, h:h + 1]
            cum = cum_all[:, C_HEADS + h:C_HEADS + h + 1]
            ecum = ecum_all[:, C_HEADS + h:C_HEADS + h + 1]
            qt_ref[b, :, cs] = (q * ecum).astype(BF16)
            rhs = jnp.concatenate([v * beta, k * (beta * ecum)], axis=-1)
            for j in range(tt // c):
                rows = slice(j * c, (j + 1) * c)
                cum_j = cum[rows]
                cum_row = cum_rows[C_HEADS + h:C_HEADS + h + 1, rows]
                dec = jnp.where(incl, jnp.exp(jnp.where(incl, cum_j - cum_row, 0.0)), 0.0)
                k_j = k[rows]
                kb = k_j.astype(BF16)
                last = cum_j[c - 1:c, :]
                kt_ref[b, rows, cs] = (k_j * jnp.exp(last - cum_j)).astype(BF16)
                units.append(dict(b=b, rows=rows, cs=cs, qcols=slice(h * c, (h + 1) * c), dec=dec, kb=kb,
                                  qb=q[rows].astype(BF16), beta=beta[rows], rhs=rhs[rows]))
    for un in units:
        un["kk"] = _mm_nt(un["kb"], un["kb"])
        un["qk"] = _mm_nt(un["qb"], un["kb"])
    for un in units:
        qk_ref[un["b"], un["rows"], un["qcols"]] = (un["qk"] * un["dec"]).astype(BF16)
        n = jnp.where(r > s, un["beta"] * un["kk"] * un["dec"], 0.0)
        un["p"] = n
        un["corr"] = -n
    for _ in range(int(math.log2(c)) - 1):
        for un in units:
            pb = un["p"].astype(BF16)
            un["p"] = jnp.dot(pb, pb, preferred_element_type=F32)
        for un in units:
            un["corr"] = un["corr"] + un["p"] + _mm(un["corr"], un["p"])
    for un in units:
        un["sol"] = un["rhs"] + _mm(un["corr"], un["rhs"])
    for un in units:
        sv_ref[un["b"], un["rows"], un["cs"]] = un["sol"][:, 0:C_DIM]
        sk_ref[un["b"], un["rows"], un["cs"]] = un["sol"][:, C_DIM:].astype(BF16)


def _gdn_scan_kernel(sv_ref, sk_ref, qk_ref, qt_ref, kt_ref, ec_ref, z_ref, w_ref, s0_ref,
                     o_ref, s_ref, st_ref, *, c):
    tb, tt, _ = sv_ref.shape
    ti = pl.program_id(1)

    @pl.when(ti == 0)
    def _():
        st_ref[...] = s0_ref[...]

    heads = [(b, h, slice(h * C_DIM, (h + 1) * C_DIM)) for b in range(tb) for h in range(C_HEADS)]
    for j in range(tt // c):
        rows = slice(j * c, (j + 1) * c)
        st = [st_ref[b, h] for b, h, _ in heads]
        stb = [a.astype(BF16) for a in st]
        both = [_mm(jnp.concatenate([sk_ref[b, rows, cs], qt_ref[b, rows, cs]], axis=0), stb[i])
                for i, (b, h, cs) in enumerate(heads)]
        sks = [a[0:c] for a in both]
        qs = [a[c:] for a in both]
        wb = [(sv_ref[b, rows, cs] - sks[i]).astype(BF16) for i, (b, h, cs) in enumerate(heads)]
        upd = [_mm_tn(kt_ref[b, rows, cs], wb[i]) for i, (b, h, cs) in enumerate(heads)]
        qkw = [_mm(qk_ref[b, rows, h * c:(h + 1) * c], wb[i]) for i, (b, h, cs) in enumerate(heads)]
        for i, (b, h, cs) in enumerate(heads):
            e_last = ec_ref[b, (j + 1) * c - 1:(j + 1) * c, C_HEADS + h:C_HEADS + h + 1]
            st_ref[b, h] = e_last * st[i] + upd[i]
            o_ref[b, rows, cs] = (_rms_scaled(qs[i] + qkw[i], w_ref[...]) * z_ref[b, rows, cs]).astype(o_ref.dtype)

    @pl.when(ti == pl.num_programs(1) - 1)
    def _():
        s_ref[...] = st_ref[...]


def _gdn(sc, conv_w, a_log, dt_bias, norm_w, state, buf, sl, tb_prep, tb_scan, tt):
    bsz, t, _ = sc.shape
    c = min(CHUNK, t)
    keep = C_CONV - 1
    assert t >= keep and tt % c == 0 and t % tt == 0
    pad_vec = lambda a: jnp.zeros((1, LANES), F32).at[0, C_HEADS:2 * C_HEADS].set(a.astype(F32))
    const = lambda shape: pl.BlockSpec(shape, lambda b, i: (0,) * len(shape))
    tile = lambda tb, w, col=0: pl.BlockSpec((tb, tt, w), lambda b, i: (b, i, col))
    shape = lambda w, dt: jax.ShapeDtypeStruct((bsz, t, w), dt)
    small_col = (C_QKV + C_WIDTH) // LANES
    prev_spec = pl.BlockSpec((tb_prep, SUBLANES, C_QKV),
                             lambda b, i: (b, jnp.maximum(i * (tt // SUBLANES) - 1, 0), 0))
    sv, sk, qk, qt, kt, ec = pl.pallas_call(
        functools.partial(_gdn_prep_kernel, c=c),
        grid=(bsz // tb_prep, t // tt),
        in_specs=[tile(tb_prep, C_QKV), tile(tb_prep, LANES, small_col), prev_spec,
                  _layer_state(sl, tb_prep, keep, C_QKV),
                  const((C_CONV, C_QKV)), const((1, LANES)), const((1, LANES))],
        out_specs=[tile(tb_prep, C_WIDTH), tile(tb_prep, C_WIDTH), tile(tb_prep, C_HEADS * c),
                   tile(tb_prep, C_WIDTH), tile(tb_prep, C_WIDTH), tile(tb_prep, LANES)],
        out_shape=[shape(C_WIDTH, F32), shape(C_WIDTH, BF16), shape(C_HEADS * c, BF16),
                   shape(C_WIDTH, BF16), shape(C_WIDTH, BF16), shape(LANES, F32)],
        scratch_shapes=[pltpu.VMEM((tb_prep, SUBLANES + tt, C_QKV), F32)],
        compiler_params=_cparams("parallel", "parallel"),
        name="gdn_prep",
    )(sc, sc, sc, buf, conv_w, pad_vec(a_log), pad_vec(dt_bias))
    st_spec = pl.BlockSpec((tb_scan, C_HEADS, C_DIM, C_DIM), lambda b, i: (b, 0, 0, 0))
    o, new_state = pl.pallas_call(
        functools.partial(_gdn_scan_kernel, c=c),
        grid=(bsz // tb_scan, t // tt),
        in_specs=[tile(tb_scan, C_WIDTH), tile(tb_scan, C_WIDTH), tile(tb_scan, C_HEADS * c),
                  tile(tb_scan, C_WIDTH), tile(tb_scan, C_WIDTH), tile(tb_scan, LANES),
                  tile(tb_scan, C_WIDTH, C_QKV // C_WIDTH), const((1, C_DIM)),
                  _layer_state(sl, tb_scan, C_HEADS, C_DIM, C_DIM)],
        out_specs=[tile(tb_scan, C_WIDTH), st_spec],
        out_shape=[shape(C_WIDTH, BF16), jax.ShapeDtypeStruct((bsz, C_HEADS, C_DIM, C_DIM), F32)],
        scratch_shapes=[pltpu.VMEM((tb_scan, C_HEADS, C_DIM, C_DIM), F32)],
        compiler_params=_cparams("parallel", "arbitrary"),
        name="gdn_scan",
    )(sv, sk, qk, qt, kt, ec, sc, norm_w.reshape(1, C_DIM), state)
    return o, new_state, sc[:, t - keep:, 0:C_QKV]


def _merge_rows(oa, ob, oc, gates, x, wa, wb, wc, wo, ln_g, ln_b, alpha):
    d = x.shape[-1]
    mix = gates[:, 0:d].astype(F32) * _mm(oa, wa)
    mix = mix + gates[:, d:2 * d].astype(F32) * _mm(ob, wb)
    mix = mix + gates[:, 2 * d:3 * d].astype(F32) * _mm(oc, wc)
    return _ln_rows(alpha * x + _mm(mix, wo), ln_g, ln_b)


def _ffn_kernel(oa_ref, ob_ref, oc_ref, g_ref, x_ref, wa_ref, wb_ref, wc_ref, wo_ref, l1g_ref, l1b_ref,
                b0_ref, wu_ref, cw_ref, cb_ref, wd_ref, lg_ref, lb_ref,
                yf_ref, yb_ref, b_ref, carry_ref, act_ref, *, alpha):
    tb, tt, d = x_ref.shape
    d_ff = wd_ref.shape[0]
    m = tb * tt
    nj = d_ff // FFN_COLS
    keep = F_CONV - 1
    pad = SUBLANES
    ti = pl.program_id(1)

    @pl.when(ti == 0)
    def _():
        carry_ref[:, 0:pad - keep, :] = jnp.zeros((tb, pad - keep, carry_ref.shape[-1]), F32)
        carry_ref[:, pad - keep:, :] = b0_ref[...]

    flat = lambda ref: ref[...].reshape(m, ref.shape[-1])
    x = _merge_rows(flat(oa_ref), flat(ob_ref), flat(oc_ref), flat(g_ref), flat(x_ref),
                    wa_ref[...], wb_ref[...], wc_ref[...], wo_ref[...], l1g_ref[...], l1b_ref[...], alpha)
    xb = x.astype(BF16)

    def up(c0):
        return jnp.dot(xb, wu_ref[:, c0:c0 + FFN_COLS], preferred_element_type=F32)

    def conv(c0, rows):
        cols = slice(c0, c0 + FFN_COLS)
        rows = rows.reshape(tb, tt, FFN_COLS)
        win = jnp.concatenate([carry_ref[:, :, cols], rows], axis=1).reshape(tb * (pad + tt), FFN_COLS)
        out = win * cw_ref[keep:keep + 1, cols]
        for sft in range(1, F_CONV):
            out = out + _shift_rows(win, sft) * cw_ref[keep - sft:keep - sft + 1, cols]
        carry_ref[:, pad - keep:, cols] = rows[:, tt - keep:, :]
        return out.reshape(tb, pad + tt, FFN_COLS)[:, pad:, :] + cb_ref[:, cols]

    parts = [0, (nj + 2) // 3, (2 * nj + 2) // 3, nj]
    down = None
    nxt = (up(0), up(d_ff))
    for j in range(nj):
        cur = nxt
        if j + 1 < nj:
            nxt = (up((j + 1) * FFN_COLS), up(d_ff + (j + 1) * FFN_COLS))
        gate = conv(j * FFN_COLS, cur[0])
        val = conv(d_ff + j * FFN_COLS, cur[1])
        act_ref[:, j * FFN_COLS:(j + 1) * FFN_COLS] = (_silu(gate) * val).reshape(m, FFN_COLS).astype(BF16)
        if j + 1 in parts:
            lo = parts[parts.index(j + 1) - 1] * FFN_COLS
            part = jnp.dot(act_ref[:, lo:(j + 1) * FFN_COLS], wd_ref[lo:(j + 1) * FFN_COLS, :],
                           preferred_element_type=F32)
            down = part if down is None else down + part
    y = _ln_rows(alpha * x + down, lg_ref[...], lb_ref[...]).reshape(tb, tt, d)
    yf_ref[...] = y
    yb_ref[...] = y.astype(BF16)

    @pl.when(ti == pl.num_programs(1) - 1)
    def _():
        b_ref[...] = carry_ref[:, pad - keep:, :]


def _merge_ffn(oa, ob, oc, gates, x, w_branch, wout, ln1_g, ln1_b,
               buf, sl, wu, cw, cb, wd, layer, ln_g, ln_b, alpha, tb, tt):
    bsz, t, d = x.shape
    d_ff = wd.shape[1]
    keep = F_CONV - 1
    assert A_WIDTH == B_WIDTH == C_WIDTH
    tile = lambda w: pl.BlockSpec((tb, tt, w), lambda b, i: (b, i, 0))
    row = tile(d)
    branch = lambda k: _resident(w_branch, layer, row_block=k, rows=A_WIDTH)
    buf_spec = pl.BlockSpec((tb, keep, 2 * d_ff), lambda b, i: (b, 0, 0))
    const = lambda a: pl.BlockSpec(a.shape, lambda b, i: (0,) * a.ndim, pipeline_mode=pl.Buffered(1))
    l1g = ln1_g.reshape(1, d)
    l1b = ln1_b.reshape(1, d)
    lg = ln_g.reshape(1, d)
    lb = ln_b.reshape(1, d)
    cb = cb.reshape(1, 2 * d_ff)
    return pl.pallas_call(
        functools.partial(_ffn_kernel, alpha=alpha),
        grid=(bsz // tb, t // tt),
        in_specs=[tile(A_WIDTH), tile(B_WIDTH), tile(C_WIDTH), tile(N_BRANCH * d), row,
                  branch(0), branch(1), branch(2), _resident(wout, layer), const(l1g), const(l1b),
                  _layer_state(sl, tb, keep, 2 * d_ff), _resident(wu, layer), const(cw), const(cb),
                  _resident(wd, layer),
                  const(lg), const(lb)],
        out_specs=[row, row, buf_spec],
        out_shape=[jax.ShapeDtypeStruct((bsz, t, d), F32),
                   jax.ShapeDtypeStruct((bsz, t, d), BF16),
                   jax.ShapeDtypeStruct((bsz, keep, 2 * d_ff), F32)],
        scratch_shapes=[pltpu.VMEM((tb, SUBLANES, 2 * d_ff), F32),
                        pltpu.VMEM((tb * tt, d_ff), BF16)],
        compiler_params=_cparams("parallel", "arbitrary"),
        name="merge_ffn",
    )(oa, ob, oc, gates, x, w_branch, w_branch, w_branch, wout, l1g, l1b, buf, wu, cw, cb, wd, lg, lb)


def _layer(xf, xb, state, sl, p, prompt, alpha):
    k_cache, v_cache, s_hgrn, s_gdn, buf_gdn, buf_ffn = state
    bsz, t, d = xf.shape
    n = bsz * t
    x2 = xb.reshape(n, d)
    sq, skv, sb, sc, sg = _in_proj(x2, p["w_in"], p["layer"])
    sq, skv, sb, sc = (a.reshape(bsz, t, -1) for a in (sq, skv, sb, sc))
    oa, new_k, new_v = _swa(sq, skv, k_cache, v_cache, p["sink"], prompt)
    ob, new_hgrn = _hgrn(sb, p["lb"], p["hgrn_w"], s_hgrn, sl, *((1, 256) if prompt else (8, t)))
    gdn_tiles = (1, bsz, 256) if prompt else (8, 8, t)
    oc, new_gdn, new_gconv = _gdn(sc, p["gdn_conv_w"], p["a_log"], p["dt_bias"], p["gdn_w"], s_gdn, buf_gdn,
                                  sl, *gdn_tiles)
    tb, tt = (1, 256) if prompt else (8, t)
    zf, zb, new_fconv = _merge_ffn(oa, ob, oc, sg.reshape(bsz, t, -1), xf, p["w_branch"], p["wout"],
                                   p["ln1_g"], p["ln1_b"], buf_ffn, sl, p["wu"], p["fcw"], p["fcb"], p["wd"],
                                   p["layer"], p["ln2_g"], p["ln2_b"], alpha, tb, tt)
    return zf, zb, (new_k, new_v, new_hgrn, new_gdn, new_gconv, new_fconv)


def kernel(x_prompt, x_sample, cache_swa_k, cache_swa_v, state_hgrn, state_gdn, state_gdn_conv, state_ffn_conv, ln_in_g, ln_in_b, w_in, attn_sinks, hgrn_lb_logits, hgrn_norm_w, gdn_conv_w, gdn_a_log, gdn_dt_bias, gdn_norm_w, w_branch, w_out, ln1_g, ln1_b, w_up, ffn_conv_w, ffn_conv_b, w_down, ln2_g, ln2_b):
    depth, d, _ = w_in.shape
    alpha = (2 * depth) ** 0.25
    lb_sm = jax.nn.softmax(hgrn_lb_logits.astype(F32), axis=0)
    lower_bounds = jnp.cumsum(lb_sm, axis=0) - lb_sm[0]

    o_a = A_WIDTH + 2 * A_KV_WIDTH
    o_b = o_a + 4 * B_WIDTH
    o_c = o_b + C_QKV + C_WIDTH
    o_s = o_c + 2 * C_HEADS
    wq = w_in[:, :, 0:A_WIDTH].astype(BF16)
    wkv = w_in[:, :, A_WIDTH:o_a].astype(BF16)
    wb = w_in[:, :, o_a:o_b].astype(BF16)
    wc = jnp.concatenate([w_in[:, :, o_b:o_s], jnp.zeros((depth, d, LANES - 2 * C_HEADS), w_in.dtype)],
                         axis=-1).astype(BF16)
    wg = w_in[:, :, o_s:].astype(BF16)
    w_in_groups = (wq, wkv, wb, wc, wg)
    w_branch_b = w_branch.astype(BF16)
    wout = w_out.astype(BF16)
    wu, fcw, fcb, wd = w_up.astype(BF16), ffn_conv_w.astype(F32), ffn_conv_b.astype(F32), w_down.astype(BF16)

    hp_f, hp_b = _layer_norm(x_prompt, ln_in_g, ln_in_b)
    hs_f, hs_b = _layer_norm(x_sample, ln_in_g, ln_in_b)
    bp = x_prompt.shape[0]
    prompt_state = (None, None,
                    jnp.zeros((1, bp, B_HEADS, B_DIM, B_DIM), F32),
                    jnp.zeros((1, bp, C_HEADS, C_DIM, C_DIM), F32),
                    jnp.zeros((1, bp, C_CONV - 1, C_QKV), F32),
                    jnp.zeros((1, bp, F_CONV - 1, w_up.shape[-1]), F32))
    p_new, s_new = [], []
    for l in range(depth):
        p = dict(layer=l, w_in=w_in_groups, sink=attn_sinks[l].astype(F32), lb=lower_bounds[l],
                 hgrn_w=hgrn_norm_w[l], gdn_conv_w=gdn_conv_w[l], a_log=gdn_a_log[l], dt_bias=gdn_dt_bias[l],
                 gdn_w=gdn_norm_w[l], w_branch=w_branch_b, wout=wout,
                 ln1_g=ln1_g[l], ln1_b=ln1_b[l], wu=wu, fcw=fcw[l], fcb=fcb[l], wd=wd,
                 ln2_g=ln2_g[l], ln2_b=ln2_b[l])
        hp_f, hp_b, st_p = _layer(hp_f, hp_b, prompt_state, 0, p, True, alpha)
        sample_state = (cache_swa_k[l], cache_swa_v[l], state_hgrn, state_gdn, state_gdn_conv, state_ffn_conv)
        hs_f, hs_b, st_s = _layer(hs_f, hs_b, sample_state, l, p, False, alpha)
        p_new.append(st_p)
        s_new.append(st_s)

    def stack(states, i):
        return jnp.stack([st[i] for st in states])

    return (hp_f, hs_f) + tuple(stack(p_new, i) for i in range(6)) + tuple(stack(s_new, i) for i in range(6))
```
